```python
import jax, jax.numpy as jnp
from jax import lax
import numpy as np

D_MODEL = 1024
BATCH = 8
SEQ = 2048
DEPTH = 1

RET_HEADS = 4
RET_HEAD_DIM = 128
RET_WIDTH = RET_HEADS * RET_HEAD_DIM
RET_CHUNK = 128
ROPE_BASE = 10000.0
SGU_GROUPS = 4
SGU_GROUP_DIM = 128
SGU_WIDTH = SGU_GROUPS * SGU_GROUP_DIM
SGU_CHUNK = 128
MIX_WIDTH = RET_WIDTH + SGU_WIDTH
PROJ_WIDTH = 4 * RET_WIDTH + 2 * SGU_WIDTH
D_FF = 2816
CONV_WIDTH = 3
EPS = 1e-6

kernel_name = "hymba_style_retention_sgu_convffn"


def rmsnorm(x, g):
    xf = x.astype(jnp.float32)
    y = xf * lax.rsqrt(jnp.mean(xf * xf, axis=-1, keepdims=True) + EPS)
    return (y * g.astype(jnp.float32)).astype(x.dtype)


def layernorm(x, g, b):
    xf = x.astype(jnp.float32)
    mu = jnp.mean(xf, axis=-1, keepdims=True)
    xc = xf - mu
    y = xc * lax.rsqrt(jnp.mean(xc * xc, axis=-1, keepdims=True) + EPS)
    return (y * g.astype(jnp.float32) + b.astype(jnp.float32)).astype(x.dtype)


def rotary(x, cos, sin):
    half = x.shape[-1] // 2
    x1, x2 = x[..., :half], x[..., half:]
    c = cos[None, :, None, :]
    s = sin[None, :, None, :]
    return jnp.concatenate([x1 * c - x2 * s, x2 * c + x1 * s], axis=-1)


def retention_chunkwise(q, k, v):
    B, S, H, D = q.shape
    C = RET_CHUNK
    N = S // C
    dt = q.dtype
    log_gamma = jnp.log(1.0 - jnp.power(2.0, -5.0 - jnp.arange(H, dtype=jnp.float32)))
    pos = jnp.arange(C, dtype=jnp.float32)
    diff = pos[:, None] - pos[None, :]
    decay_mask = jnp.where(diff >= 0.0,
                           jnp.exp(log_gamma[:, None, None] * jnp.maximum(diff, 0.0)[None]),
                           0.0).astype(dt)
    k_decay = jnp.exp(log_gamma[:, None] * (C - 1.0 - pos)[None]).astype(dt)
    q_decay = jnp.exp(log_gamma[:, None] * (pos + 1.0)[None]).astype(dt)
    chunk_decay = jnp.exp(log_gamma * C).astype(dt)

    def to_chunks(t):
        return t.reshape(B, N, C, H, D).transpose(0, 3, 1, 2, 4)

    qc, kc, vc = to_chunks(q), to_chunks(k), to_chunks(v)
    scores = jnp.einsum('bhnqd,bhnkd->bhnqk', qc, kc) * decay_mask[None, :, None]
    intra = jnp.einsum('bhnqk,bhnkd->bhnqd', scores, vc)
    kv = jnp.einsum('bhnkd,bhnke->bhnde', kc * k_decay[None, :, None, :, None], vc)

    def step(state, kv_n):
        return state * chunk_decay[None, :, None, None] + kv_n, state

    init = jnp.zeros((B, H, D, D), dtype=kv.dtype)
    _, s_prev = lax.scan(step, init, jnp.moveaxis(kv, 2, 0))
    s_prev = jnp.moveaxis(s_prev, 0, 2)
    cross = jnp.einsum('bhnqd,bhnde->bhnqe', qc * q_decay[None, :, None, :, None], s_prev)
    out = intra + cross
    return out.transpose(0, 2, 3, 1, 4).reshape(B, S, H, D)


def spatial_gating_chunked(u, v, ln_g, ln_b, w_s, b_s):
    B, S, _ = u.shape
    C = SGU_CHUNK
    N = S // C
    G, dg = SGU_GROUPS, SGU_GROUP_DIM
    vn = layernorm(v.reshape(B, N, C, G, dg), ln_g, ln_b)
    causal = jnp.tril(jnp.ones((C, C), dtype=w_s.dtype))
    w = w_s * causal[None]
    mixed = jnp.einsum('gts,bnsgd->bntgd', w, vn) + b_s.T[None, None, :, :, None]
    return u * mixed.reshape(B, S, G * dg)


def causal_depthwise_conv(h, w, b):
    S = h.shape[1]
    hp = jnp.pad(h, ((0, 0), (CONV_WIDTH - 1, 0), (0, 0)))
    y = hp[:, 0:S] * w[0]
    for j in range(1, CONV_WIDTH):
        y = y + hp[:, j:j + S] * w[j]
    return y + b


def setup_inputs(seed: int = 0) -> dict:
    key = jax.random.key(seed)
    ks = jax.random.split(key, 16)
    f32 = jnp.float32
    nrm = lambda k, shape, scale: jax.random.normal(k, shape, f32) * scale
    return {
        "x": nrm(ks[0], (BATCH, SEQ, D_MODEL), 1.0),
        "mix_norm_g": 1.0 + nrm(ks[1], (DEPTH, D_MODEL), 0.01),
        "w_in": nrm(ks[2], (DEPTH, D_MODEL, PROJ_WIDTH), D_MODEL ** -0.5),
        "ret_norm_g": 1.0 + nrm(ks[3], (DEPTH, RET_WIDTH), 0.01),
        "sgu_ln_g": 1.0 + nrm(ks[4], (DEPTH, SGU_GROUPS, SGU_GROUP_DIM), 0.01),
        "sgu_ln_b": nrm(ks[5], (DEPTH, SGU_GROUPS, SGU_GROUP_DIM), 0.01),
        "sgu_w_s": nrm(ks[6], (DEPTH, SGU_GROUPS, SGU_CHUNK, SGU_CHUNK), SGU_CHUNK ** -0.5),
        "sgu_b_s": 1.0 + nrm(ks[7], (DEPTH, SGU_GROUPS, SGU_CHUNK), 0.01),
        "w_out": nrm(ks[8], (DEPTH, MIX_WIDTH, D_MODEL), MIX_WIDTH ** -0.5),
        "ffn_norm_g": 1.0 + nrm(ks[9], (DEPTH, D_MODEL), 0.01),
        "w_up": nrm(ks[10], (DEPTH, D_MODEL, 2 * D_FF), D_MODEL ** -0.5),
        "conv_w": nrm(ks[11], (DEPTH, CONV_WIDTH, 2 * D_FF), CONV_WIDTH ** -0.5),
        "conv_b": nrm(ks[12], (DEPTH, 2 * D_FF), 0.01),
        "w_down": nrm(ks[13], (DEPTH, D_FF, D_MODEL), D_FF ** -0.5),
        "final_norm_g": 1.0 + nrm(ks[14], (D_MODEL,), 0.01),
    }


def reference(x, mix_norm_g, w_in, ret_norm_g, sgu_ln_g, sgu_ln_b, sgu_w_s, sgu_b_s,
              w_out, ffn_norm_g, w_up, conv_w, conv_b, w_down, final_norm_g):
    B, S, _ = x.shape
    half = RET_HEAD_DIM // 2
    inv_freq = jnp.power(ROPE_BASE, -jnp.arange(half, dtype=jnp.float32) / half)
    ang = jnp.arange(S, dtype=jnp.float32)[:, None] * inv_freq[None, :]
    cos, sin = jnp.cos(ang).astype(x.dtype), jnp.sin(ang).astype(x.dtype)
    splits = [RET_WIDTH, 2 * RET_WIDTH, 3 * RET_WIDTH, 4 * RET_WIDTH, 4 * RET_WIDTH + SGU_WIDTH]

    for l in range(DEPTH):
        h = rmsnorm(x, mix_norm_g[l])
        proj = h @ w_in[l]
        q, k, v, g, u, sv = jnp.split(proj, splits, axis=-1)
        q = rotary(q.reshape(B, S, RET_HEADS, RET_HEAD_DIM), cos, sin)
        k = rotary(k.reshape(B, S, RET_HEADS, RET_HEAD_DIM), cos, sin) * (RET_HEAD_DIM ** -0.5)
        v = v.reshape(B, S, RET_HEADS, RET_HEAD_DIM)
        ret = retention_chunkwise(q, k, v)
        ret = rmsnorm(ret, ret_norm_g[l].reshape(RET_HEADS, RET_HEAD_DIM)).reshape(B, S, RET_WIDTH)
        ret = jax.nn.silu(g) * ret
        sgu = spatial_gating_chunked(jax.nn.gelu(u, approximate=False),
                                     jax.nn.gelu(sv, approximate=False),
                                     sgu_ln_g[l], sgu_ln_b[l], sgu_w_s[l], sgu_b_s[l])
        mixed = jnp.concatenate([ret, sgu], axis=-1) @ w_out[l]
        x = x + mixed
        h = rmsnorm(x, ffn_norm_g[l])
        up = causal_depthwise_conv(h @ w_up[l], conv_w[l], conv_b[l])
        a, bgate = jnp.split(up, [D_FF], axis=-1)
        x = x + (jax.nn.silu(a) * bgate) @ w_down[l]

    return rmsnorm(x, final_norm_g)
```

```python
import jax
import jax.numpy as jnp
import numpy as np
from jax import lax
from jax.experimental import pallas as pl
from jax.experimental.pallas import tpu as pltpu

D_MODEL = 1024
RET_HEADS = 4
HEAD_DIM = 128
RET_WIDTH = RET_HEADS * HEAD_DIM
SGU_GROUPS = 4
GROUP_DIM = 128
SGU_WIDTH = SGU_GROUPS * GROUP_DIM
CHUNK = 128
ROPE_BASE = 10000.0
D_FF = 2816
CONV_WIDTH = 3
EPS = 1e-6

SEQ_TILE = 256
FF_BLOCK = 256
CARRY_ROWS = 8
VMEM_LIMIT_BYTES = 56 * 1024 * 1024

_F32 = jnp.float32
_BF16 = jnp.bfloat16
_SQRT_HALF = float(np.sqrt(0.5))


def _rms_scale(x):
    return lax.rsqrt(jnp.mean(x * x, axis=-1, keepdims=True) + EPS)


def _gelu(x):
    return 0.5 * x * (1.0 + lax.erf(x * _SQRT_HALF))


def _rotary(t, cos_full, sin_signed):
    return t * cos_full + pltpu.roll(t, HEAD_DIM // 2, axis=1) * sin_signed


def _mix_kernel(x_ref, g_ref, win_ref, cos_ref, sin_ref, cosk_ref, sink_ref,
                mask_ref, kdec_ref, qdec_ref, cdec_ref, retg_ref, lng_ref, lnb_ref,
                ws_ref, bs_ref, wout_ref, o_ref, state_ref, mix_ref):
    rows = x_ref.shape[1]

    @pl.when(pl.program_id(1) == 0)
    def _():
        state_ref[...] = jnp.zeros_like(state_ref)

    x = x_ref[0]
    h = (x * _rms_scale(x) * g_ref[...]).astype(_BF16)

    def proj(lo, hi):
        return jnp.dot(h, win_ref[:, lo:hi], preferred_element_type=_F32)

    q = proj(0, RET_WIDTH)
    k = proj(RET_WIDTH, 2 * RET_WIDTH)
    v = proj(2 * RET_WIDTH, 3 * RET_WIDTH)
    g = proj(3 * RET_WIDTH, 4 * RET_WIDTH)
    u = proj(4 * RET_WIDTH, 4 * RET_WIDTH + SGU_WIDTH)
    sv = proj(4 * RET_WIDTH + SGU_WIDTH, 4 * RET_WIDTH + 2 * SGU_WIDTH)

    cos, sin = cos_ref[...], sin_ref[...]
    cosk, sink = cosk_ref[...], sink_ref[...]

    for c in range(rows // CHUNK):
        r = slice(c * CHUNK, (c + 1) * CHUNK)
        for hd in range(RET_HEADS):
            l = slice(hd * HEAD_DIM, (hd + 1) * HEAD_DIM)
            qh = _rotary(q[r, l], cos[r], sin[r])
            kh = _rotary(k[r, l], cosk[r], sink[r])
            vh = v[r, l].astype(_BF16)
            qb = qh.astype(_BF16)
            scores = lax.dot_general(qb, kh.astype(_BF16), (((1,), (1,)), ((), ())),
                                     preferred_element_type=_F32) * mask_ref[hd]
            intra = jnp.dot(scores.astype(_BF16), vh, preferred_element_type=_F32)
            state = state_ref[hd]
            cross = jnp.dot((qh * qdec_ref[:, l]).astype(_BF16), state.astype(_BF16),
                            preferred_element_type=_F32)
            kv = lax.dot_general((kh * kdec_ref[:, l]).astype(_BF16), vh,
                                 (((0,), (0,)), ((), ())), preferred_element_type=_F32)
            state_ref[hd] = state * cdec_ref[:, l] + kv
            ret = intra + cross
            ret = ret * _rms_scale(ret) * retg_ref[:, l]
            gh = g[r, l]
            mix_ref[r, l] = (gh * jax.nn.sigmoid(gh) * ret).astype(_BF16)
        for gr in range(SGU_GROUPS):
            l = slice(gr * GROUP_DIM, (gr + 1) * GROUP_DIM)
            gv = _gelu(sv[r, l])
            xc = gv - jnp.mean(gv, axis=-1, keepdims=True)
            vn = xc * _rms_scale(xc) * lng_ref[:, l] + lnb_ref[:, l]
            mixed = jnp.dot(ws_ref[gr], vn.astype(_BF16), preferred_element_type=_F32)
            out = _gelu(u[r, l]) * (mixed + bs_ref[:, l])
            mix_ref[r, SGU_WIDTH + gr * GROUP_DIM:SGU_WIDTH + (gr + 1) * GROUP_DIM] = out.astype(_BF16)

    o_ref[0] = x + jnp.dot(mix_ref[...], wout_ref[...], preferred_element_type=_F32)


def _ffn_kernel(x_ref, g_ref, wup_ref, cw_ref, cb_ref, wdown_ref, gf_ref, o_ref,
                up_ref, act_ref):
    rows = x_ref.shape[1]
    first = pl.program_id(1) == 0

    @pl.when(first)
    def _():
        up_ref[0:CARRY_ROWS, :] = jnp.zeros((CARRY_ROWS, up_ref.shape[1]), _F32)

    @pl.when(jnp.logical_not(first))
    def _():
        up_ref[0:CARRY_ROWS, :] = up_ref[rows:rows + CARRY_ROWS, :]

    x = x_ref[0]
    h = (x * _rms_scale(x) * g_ref[...]).astype(_BF16)

    def conv(lo, hi):
        up_ref[CARRY_ROWS:CARRY_ROWS + rows, lo:hi] = jnp.dot(
            h, wup_ref[:, lo:hi], preferred_element_type=_F32)
        y = cb_ref[:, lo:hi]
        for tap in range(CONV_WIDTH):
            start = CARRY_ROWS - (CONV_WIDTH - 1) + tap
            y = y + up_ref[start:start + rows, lo:hi] * cw_ref[tap:tap + 1, lo:hi]
        return y

    for blk in range(D_FF // FF_BLOCK):
        lo, hi = blk * FF_BLOCK, (blk + 1) * FF_BLOCK
        a = conv(lo, hi)
        b = conv(D_FF + lo, D_FF + hi)
        act_ref[:, lo:hi] = (a * jax.nn.sigmoid(a) * b).astype(_BF16)

    y = x + jnp.dot(act_ref[...], wdown_ref[...], preferred_element_type=_F32)
    o_ref[0] = y * _rms_scale(y) * gf_ref[...]


def _const_spec(shape):
    return pl.BlockSpec(shape, lambda b, j: (0,) * len(shape), pipeline_mode=pl.Buffered(1))


def _decay_tables():
    log_gamma = jnp.log(1.0 - jnp.power(2.0, -5.0 - jnp.arange(RET_HEADS, dtype=_F32)))
    pos = jnp.arange(CHUNK, dtype=_F32)
    diff = pos[:, None] - pos[None, :]
    mask = jnp.where(diff >= 0.0,
                     jnp.exp(log_gamma[:, None, None] * jnp.maximum(diff, 0.0)[None]), 0.0)
    k_decay = jnp.exp(log_gamma[:, None] * (CHUNK - 1.0 - pos)[None])
    q_decay = jnp.exp(log_gamma[:, None] * (pos + 1.0)[None])
    chunk_decay = jnp.exp(log_gamma * CHUNK)
    widen = lambda t: jnp.repeat(t.T, HEAD_DIM, axis=1)
    return mask, widen(k_decay), widen(q_decay), jnp.repeat(chunk_decay, HEAD_DIM)[None, :]


def _rope_tables(seq):
    half = HEAD_DIM // 2
    inv_freq = jnp.power(ROPE_BASE, -jnp.arange(half, dtype=_F32) / half)
    ang = jnp.arange(seq, dtype=_F32)[:, None] * inv_freq[None, :]
    cos, sin = jnp.cos(ang), jnp.sin(ang)
    cos_full = jnp.concatenate([cos, cos], axis=-1)
    sin_signed = jnp.concatenate([-sin, sin], axis=-1)
    scale = HEAD_DIM ** -0.5
    return cos_full, sin_signed, cos_full * scale, sin_signed * scale


def kernel(x, mix_norm_g, w_in, ret_norm_g, sgu_ln_g, sgu_ln_b, sgu_w_s, sgu_b_s, w_out,
           ffn_norm_g, w_up, conv_w, conv_b, w_down, final_norm_g):
    batch, seq, d_model = x.shape
    assert w_in.shape[0] == 1
    assert d_model == D_MODEL and seq % SEQ_TILE == 0 and SEQ_TILE % CHUNK == 0
    assert D_FF % FF_BLOCK == 0
    l = 0

    cos, sin, cosk, sink = _rope_tables(seq)
    mask, kdec, qdec, cdec = _decay_tables()
    causal = jnp.tril(jnp.ones((CHUNK, CHUNK), dtype=_F32))
    grid = (batch, seq // SEQ_TILE)
    x_spec = pl.BlockSpec((1, SEQ_TILE, D_MODEL), lambda b, j: (b, j, 0))
    rope_spec = pl.BlockSpec((SEQ_TILE, HEAD_DIM), lambda b, j: (j, 0))
    params = pltpu.CompilerParams(dimension_semantics=("arbitrary", "arbitrary"),
                                  vmem_limit_bytes=VMEM_LIMIT_BYTES)
    out_shape = jax.ShapeDtypeStruct(x.shape, _F32)

    x = pl.pallas_call(
        _mix_kernel,
        grid=grid,
        in_specs=[x_spec, _const_spec((1, D_MODEL)), _const_spec(w_in.shape[1:]),
                  rope_spec, rope_spec, rope_spec, rope_spec,
                  _const_spec(mask.shape), _const_spec(kdec.shape), _const_spec(qdec.shape),
                  _const_spec(cdec.shape), _const_spec((1, RET_WIDTH)),
                  _const_spec((1, SGU_WIDTH)), _const_spec((1, SGU_WIDTH)),
                  _const_spec(sgu_w_s.shape[1:]), _const_spec((CHUNK, SGU_WIDTH)),
                  _const_spec(w_out.shape[1:])],
        out_specs=x_spec,
        out_shape=out_shape,
        scratch_shapes=[pltpu.VMEM((RET_HEADS, HEAD_DIM, HEAD_DIM), _F32),
                        pltpu.VMEM((SEQ_TILE, RET_WIDTH + SGU_WIDTH), _BF16)],
        compiler_params=params,
        name="token_mix",
    )(x, mix_norm_g[l][None, :], w_in[l].astype(_BF16), cos, sin, cosk, sink,
      mask, kdec, qdec, cdec, ret_norm_g[l][None, :],
      sgu_ln_g[l].reshape(1, SGU_WIDTH), sgu_ln_b[l].reshape(1, SGU_WIDTH),
      (sgu_w_s[l] * causal[None]).astype(_BF16),
      jnp.repeat(sgu_b_s[l].T, GROUP_DIM, axis=1), w_out[l].astype(_BF16))

    return pl.pallas_call(
        _ffn_kernel,
        grid=grid,
        in_specs=[x_spec, _const_spec((1, D_MODEL)), _const_spec(w_up.shape[1:]),
                  _const_spec(conv_w.shape[1:]), _const_spec((1, 2 * D_FF)),
                  _const_spec(w_down.shape[1:]), _const_spec((1, D_MODEL))],
        out_specs=x_spec,
        out_shape=out_shape,
        scratch_shapes=[pltpu.VMEM((SEQ_TILE + CARRY_ROWS, 2 * D_FF), _F32),
                        pltpu.VMEM((SEQ_TILE, D_FF), _BF16)],
        compiler_params=params,
        name="channel_mix",
    )(x, ffn_norm_g[l][None, :], w_up[l].astype(_BF16), conv_w[l], conv_b[l][None, :],
      w_down[l].astype(_BF16), final_norm_g[None, :])
```

```python
import jax
import jax.numpy as jnp
import numpy as np
from jax import lax
from jax.experimental import pallas as pl
from jax.experimental.pallas import tpu as pltpu

D_MODEL = 1024
RET_HEADS = 4
HEAD_DIM = 128
RET_WIDTH = RET_HEADS * HEAD_DIM
SGU_GROUPS = 4
GROUP_DIM = 128
SGU_WIDTH = SGU_GROUPS * GROUP_DIM
CHUNK = 128
ROPE_BASE = 10000.0
D_FF = 2816
CONV_WIDTH = 3
EPS = 1e-6

SEQ_TILE = 256
FFN_STEPS = 32
FF_BLOCK = 256
VMEM_LIMIT_BYTES = 56 * 1024 * 1024

_F32 = jnp.float32
_BF16 = jnp.bfloat16
_SQRT_HALF = float(np.sqrt(0.5))


def _rms_scale(x):
    return lax.rsqrt(jnp.mean(x * x, axis=-1, keepdims=True) + EPS)


def _gelu(x):
    return 0.5 * x * (1.0 + lax.erf(x * _SQRT_HALF))


def _rotary(t, cos_full, sin_signed):
    return t * cos_full + pltpu.roll(t, HEAD_DIM // 2, axis=1) * sin_signed


def _mix_kernel(x_ref, g_ref, win_ref, cos_ref, sin_ref, cosk_ref, sink_ref,
                mask_ref, kdec_ref, qdec_ref, cdec_ref, retg_ref, lng_ref, lnb_ref,
                ws_ref, bs_ref, wout_ref, o_ref, state_ref, mix_ref):
    rows = x_ref.shape[1]

    @pl.when(pl.program_id(1) == 0)
    def _():
        state_ref[...] = jnp.zeros_like(state_ref)

    x = x_ref[0]
    h = (x * _rms_scale(x) * g_ref[...]).astype(_BF16)

    def proj(lo, hi):
        return jnp.dot(h, win_ref[:, lo:hi], preferred_element_type=_F32)

    q = proj(0, RET_WIDTH)
    k = proj(RET_WIDTH, 2 * RET_WIDTH)
    v = proj(2 * RET_WIDTH, 3 * RET_WIDTH)
    g = proj(3 * RET_WIDTH, 4 * RET_WIDTH)
    u = proj(4 * RET_WIDTH, 4 * RET_WIDTH + SGU_WIDTH)
    sv = proj(4 * RET_WIDTH + SGU_WIDTH, 4 * RET_WIDTH + 2 * SGU_WIDTH)

    cos, sin = cos_ref[...], sin_ref[...]
    cosk, sink = cosk_ref[...], sink_ref[...]

    for c in range(rows // CHUNK):
        r = slice(c * CHUNK, (c + 1) * CHUNK)
        for hd in range(RET_HEADS):
            l = slice(hd * HEAD_DIM, (hd + 1) * HEAD_DIM)
            qh = _rotary(q[r, l], cos[r], sin[r])
            kh = _rotary(k[r, l], cosk[r], sink[r])
            vh = v[r, l].astype(_BF16)
            qb = qh.astype(_BF16)
            scores = lax.dot_general(qb, kh.astype(_BF16), (((1,), (1,)), ((), ())),
                                     preferred_element_type=_F32) * mask_ref[hd]
            intra = jnp.dot(scores.astype(_BF16), vh, preferred_element_type=_F32)
            state = state_ref[hd]
            cross = jnp.dot((qh * qdec_ref[:, l]).astype(_BF16), state.astype(_BF16),
                            preferred_element_type=_F32)
            kv = lax.dot_general((kh * kdec_ref[:, l]).astype(_BF16), vh,
                                 (((0,), (0,)), ((), ())), preferred_element_type=_F32)
            state_ref[hd] = state * cdec_ref[:, l] + kv
            ret = intra + cross
            ret = ret * _rms_scale(ret) * retg_ref[:, l]
            gh = g[r, l]
            mix_ref[r, l] = (gh * jax.nn.sigmoid(gh) * ret).astype(_BF16)
        for gr in range(SGU_GROUPS):
            l = slice(gr * GROUP_DIM, (gr + 1) * GROUP_DIM)
            gv = _gelu(sv[r, l])
            xc = gv - jnp.mean(gv, axis=-1, keepdims=True)
            vn = xc * _rms_scale(xc) * lng_ref[:, l] + lnb_ref[:, l]
            mixed = jnp.dot(ws_ref[gr], vn.astype(_BF16), preferred_element_type=_F32)
            out = _gelu(u[r, l]) * (mixed + bs_ref[:, l])
            mix_ref[r, SGU_WIDTH + gr * GROUP_DIM:SGU_WIDTH + (gr + 1) * GROUP_DIM] = out.astype(_BF16)

    o_ref[0] = x + jnp.dot(mix_ref[...], wout_ref[...], preferred_element_type=_F32)


def _ffn_kernel(x_ref, g_ref, wup_ref, cw_ref, cb_ref, wdown_ref, gf_ref, o_ref,
                up_ref, act_ref):
    batch, steps, d = x_ref.shape
    rows = batch * steps
    carry = (CONV_WIDTH - 1) * batch
    first = pl.program_id(0) == 0

    @pl.when(first)
    def _():
        up_ref[0:carry, :] = jnp.zeros((carry, up_ref.shape[1]), _F32)

    @pl.when(jnp.logical_not(first))
    def _():
        up_ref[0:carry, :] = up_ref[rows:rows + carry, :]

    x = jnp.swapaxes(x_ref[...], 0, 1).reshape(rows, d)
    h = (x * _rms_scale(x) * g_ref[...]).astype(_BF16)

    def conv(lo, hi):
        up_ref[carry:carry + rows, lo:hi] = jnp.dot(
            h, wup_ref[:, lo:hi], preferred_element_type=_F32)
        y = cb_ref[:, lo:hi]
        for tap in range(CONV_WIDTH):
            start = tap * batch
            y = y + up_ref[start:start + rows, lo:hi] * cw_ref[tap:tap + 1, lo:hi]
        return y

    for blk in range(D_FF // FF_BLOCK):
        lo, hi = blk * FF_BLOCK, (blk + 1) * FF_BLOCK
        a = conv(lo, hi)
        b = conv(D_FF + lo, D_FF + hi)
        act_ref[:, lo:hi] = (a * jax.nn.sigmoid(a) * b).astype(_BF16)

    y = x + jnp.dot(act_ref[...], wdown_ref[...], preferred_element_type=_F32)
    y = y * _rms_scale(y) * gf_ref[...]
    o_ref[...] = jnp.swapaxes(y.reshape(steps, batch, d), 0, 1)


def _const_spec(shape):
    return pl.BlockSpec(shape, lambda b, j: (0,) * len(shape), pipeline_mode=pl.Buffered(1))


def _decay_tables():
    log_gamma = jnp.log(1.0 - jnp.power(2.0, -5.0 - jnp.arange(RET_HEADS, dtype=_F32)))
    pos = jnp.arange(CHUNK, dtype=_F32)
    diff = pos[:, None] - pos[None, :]
    mask = jnp.where(diff >= 0.0,
                     jnp.exp(log_gamma[:, None, None] * jnp.maximum(diff, 0.0)[None]), 0.0)
    k_decay = jnp.exp(log_gamma[:, None] * (CHUNK - 1.0 - pos)[None])
    q_decay = jnp.exp(log_gamma[:, None] * (pos + 1.0)[None])
    chunk_decay = jnp.exp(log_gamma * CHUNK)
    widen = lambda t: jnp.repeat(t.T, HEAD_DIM, axis=1)
    return mask, widen(k_decay), widen(q_decay), jnp.repeat(chunk_decay, HEAD_DIM)[None, :]


def _rope_tables(seq):
    half = HEAD_DIM // 2
    inv_freq = jnp.power(ROPE_BASE, -jnp.arange(half, dtype=_F32) / half)
    ang = jnp.arange(seq, dtype=_F32)[:, None] * inv_freq[None, :]
    cos, sin = jnp.cos(ang), jnp.sin(ang)
    cos_full = jnp.concatenate([cos, cos], axis=-1)
    sin_signed = jnp.concatenate([-sin, sin], axis=-1)
    scale = HEAD_DIM ** -0.5
    return cos_full, sin_signed, cos_full * scale, sin_signed * scale


def kernel(x, mix_norm_g, w_in, ret_norm_g, sgu_ln_g, sgu_ln_b, sgu_w_s, sgu_b_s, w_out,
           ffn_norm_g, w_up, conv_w, conv_b, w_down, final_norm_g):
    batch, seq, d_model = x.shape
    assert w_in.shape[0] == 1
    assert d_model == D_MODEL and seq % SEQ_TILE == 0 and SEQ_TILE % CHUNK == 0
    assert D_FF % FF_BLOCK == 0
    l = 0

    cos, sin, cosk, sink = _rope_tables(seq)
    mask, kdec, qdec, cdec = _decay_tables()
    causal = jnp.tril(jnp.ones((CHUNK, CHUNK), dtype=_F32))
    grid = (batch, seq // SEQ_TILE)
    x_spec = pl.BlockSpec((1, SEQ_TILE, D_MODEL), lambda b, j: (b, j, 0))
    rope_spec = pl.BlockSpec((SEQ_TILE, HEAD_DIM), lambda b, j: (j, 0))
    params = pltpu.CompilerParams(dimension_semantics=("arbitrary", "arbitrary"),
                                  vmem_limit_bytes=VMEM_LIMIT_BYTES)
    out_shape = jax.ShapeDtypeStruct(x.shape, _F32)

    x = pl.pallas_call(
        _mix_kernel,
        grid=grid,
        in_specs=[x_spec, _const_spec((1, D_MODEL)), _const_spec(w_in.shape[1:]),
                  rope_spec, rope_spec, rope_spec, rope_spec,
                  _const_spec(mask.shape), _const_spec(kdec.shape), _const_spec(qdec.shape),
                  _const_spec(cdec.shape), _const_spec((1, RET_WIDTH)),
                  _const_spec((1, SGU_WIDTH)), _const_spec((1, SGU_WIDTH)),
                  _const_spec(sgu_w_s.shape[1:]), _const_spec((CHUNK, SGU_WIDTH)),
                  _const_spec(w_out.shape[1:])],
        out_specs=x_spec,
        out_shape=out_shape,
        scratch_shapes=[pltpu.VMEM((RET_HEADS, HEAD_DIM, HEAD_DIM), _F32),
                        pltpu.VMEM((SEQ_TILE, RET_WIDTH + SGU_WIDTH), _BF16)],
        compiler_params=params,
        name="token_mix",
    )(x, mix_norm_g[l][None, :], w_in[l].astype(_BF16), cos, sin, cosk, sink,
      mask, kdec, qdec, cdec, ret_norm_g[l][None, :],
      sgu_ln_g[l].reshape(1, SGU_WIDTH), sgu_ln_b[l].reshape(1, SGU_WIDTH),
      (sgu_w_s[l] * causal[None]).astype(_BF16),
      jnp.repeat(sgu_b_s[l].T, GROUP_DIM, axis=1), w_out[l].astype(_BF16))

    ffn_rows = batch * FFN_STEPS
    ffn_x_spec = pl.BlockSpec((batch, FFN_STEPS, D_MODEL), lambda j: (0, j, 0))
    const1 = lambda shape: pl.BlockSpec(shape, lambda j: (0,) * len(shape),
                                        pipeline_mode=pl.Buffered(1))
    return pl.pallas_call(
        _ffn_kernel,
        grid=(seq // FFN_STEPS,),
        in_specs=[ffn_x_spec, const1((1, D_MODEL)), const1(w_up.shape[1:]),
                  const1(conv_w.shape[1:]), const1((1, 2 * D_FF)),
                  const1(w_down.shape[1:]), const1((1, D_MODEL))],
        out_specs=ffn_x_spec,
        out_shape=out_shape,
        scratch_shapes=[pltpu.VMEM((ffn_rows + (CONV_WIDTH - 1) * batch, 2 * D_FF), _F32),
                        pltpu.VMEM((ffn_rows, D_FF), _BF16)],
        compiler_params=pltpu.CompilerParams(dimension_semantics=("arbitrary",),
                                             vmem_limit_bytes=VMEM_LIMIT_BYTES),
        name="channel_mix",
    )(x, ffn_norm_g[l][None, :], w_up[l].astype(_BF16), conv_w[l], conv_b[l][None, :],
      w_down[l].astype(_BF16), final_norm_g[None, :])
```

```python
import jax
import jax.numpy as jnp
import numpy as np
from jax import lax
from jax.experimental import pallas as pl
from jax.experimental.pallas import tpu as pltpu

D_MODEL = 1024
RET_HEADS = 4
HEAD_DIM = 128
RET_WIDTH = RET_HEADS * HEAD_DIM
SGU_GROUPS = 4
GROUP_DIM = 128
SGU_WIDTH = SGU_GROUPS * GROUP_DIM
CHUNK = 128
ROPE_BASE = 10000.0
D_FF = 2816
CONV_WIDTH = 3
EPS = 1e-6

MIX_ROWS = 256
MIX_BATCH = 4
FFN_STEPS = 64
FFN_SUB_STEPS = 32
FF_BLOCK = 256
VMEM_LIMIT_BYTES = 56 * 1024 * 1024

_F32 = jnp.float32
_BF16 = jnp.bfloat16
_SQRT_HALF = float(np.sqrt(0.5))


def _rms_scale(x):
    return lax.rsqrt(jnp.mean(x * x, axis=-1, keepdims=True) + EPS)


def _gelu(x):
    return 0.5 * x * (1.0 + lax.erf(x * _SQRT_HALF))


def _rotary(t, cos_full, sin_signed):
    return t * cos_full + pltpu.roll(t, HEAD_DIM // 2, axis=1) * sin_signed


def _mix_kernel(x_ref, g_ref, win_ref, cos_ref, sin_ref, cosk_ref, sink_ref,
                mask_ref, kdec_ref, qdec_ref, cdec_ref, retg_ref, lng_ref, lnb_ref,
                ws_ref, bs_ref, wout_ref, o_ref, state_ref, mix_ref):
    n_sub, rows, _ = x_ref.shape

    @pl.when(pl.program_id(1) == 0)
    def _():
        state_ref[...] = jnp.zeros_like(state_ref)

    cos, sin = cos_ref[...], sin_ref[...]
    cosk, sink = cosk_ref[...], sink_ref[...]

    for i in range(n_sub):
        x = x_ref[i]
        h = (x * _rms_scale(x) * g_ref[...]).astype(_BF16)

        def proj(lo, hi):
            return jnp.dot(h, win_ref[:, lo:hi], preferred_element_type=_F32)

        q = proj(0, RET_WIDTH)
        k = proj(RET_WIDTH, 2 * RET_WIDTH)
        v = proj(2 * RET_WIDTH, 3 * RET_WIDTH)
        g = proj(3 * RET_WIDTH, 4 * RET_WIDTH)
        u = proj(4 * RET_WIDTH, 4 * RET_WIDTH + SGU_WIDTH)
        sv = proj(4 * RET_WIDTH + SGU_WIDTH, 4 * RET_WIDTH + 2 * SGU_WIDTH)

        for c in range(rows // CHUNK):
            r = slice(c * CHUNK, (c + 1) * CHUNK)
            for hd in range(RET_HEADS):
                l = slice(hd * HEAD_DIM, (hd + 1) * HEAD_DIM)
                qh = _rotary(q[r, l], cos[r], sin[r])
                kh = _rotary(k[r, l], cosk[r], sink[r])
                vh = v[r, l].astype(_BF16)
                scores = lax.dot_general(qh.astype(_BF16), kh.astype(_BF16),
                                         (((1,), (1,)), ((), ())),
                                         preferred_element_type=_F32) * mask_ref[hd]
                intra = jnp.dot(scores.astype(_BF16), vh, preferred_element_type=_F32)
                state = state_ref[i, hd]
                cross = jnp.dot((qh * qdec_ref[:, l]).astype(_BF16), state.astype(_BF16),
                                preferred_element_type=_F32)
                kv = lax.dot_general((kh * kdec_ref[:, l]).astype(_BF16), vh,
                                     (((0,), (0,)), ((), ())), preferred_element_type=_F32)
                state_ref[i, hd] = state * cdec_ref[:, l] + kv
                ret = intra + cross
                ret = ret * _rms_scale(ret) * retg_ref[:, l]
                gh = g[r, l]
                mix_ref[i, r, l] = (gh * jax.nn.sigmoid(gh) * ret).astype(_BF16)
            for gr in range(SGU_GROUPS):
                l = slice(gr * GROUP_DIM, (gr + 1) * GROUP_DIM)
                gv = _gelu(sv[r, l])
                xc = gv - jnp.mean(gv, axis=-1, keepdims=True)
                vn = xc * _rms_scale(xc) * lng_ref[:, l] + lnb_ref[:, l]
                mixed = jnp.dot(ws_ref[gr], vn.astype(_BF16), preferred_element_type=_F32)
                out = _gelu(u[r, l]) * (mixed + bs_ref[:, l])
                mix_ref[i, r, RET_WIDTH + gr * GROUP_DIM:RET_WIDTH + (gr + 1) * GROUP_DIM] = (
                    out.astype(_BF16))

        o_ref[i] = x + jnp.dot(mix_ref[i], wout_ref[...], preferred_element_type=_F32)


def _ffn_kernel(x_ref, g_ref, wup_ref, cw_ref, cb_ref, wdown_ref, gf_ref, o_ref,
                up_ref, act_ref):
    batch, steps, d = x_ref.shape
    sub_rows = batch * FFN_SUB_STEPS
    carry = (CONV_WIDTH - 1) * batch

    @pl.when(pl.program_id(0) == 0)
    def _():
        up_ref[sub_rows:sub_rows + carry, :] = jnp.zeros((carry, up_ref.shape[1]), _F32)

    for sub in range(steps // FFN_SUB_STEPS):
        t0 = sub * FFN_SUB_STEPS
        x = jnp.swapaxes(x_ref[:, t0:t0 + FFN_SUB_STEPS, :], 0, 1).reshape(sub_rows, d)
        h = (x * _rms_scale(x) * g_ref[...]).astype(_BF16)
        up_ref[0:carry, :] = up_ref[sub_rows:sub_rows + carry, :]

        def conv(lo, hi):
            up_ref[carry:carry + sub_rows, lo:hi] = jnp.dot(
                h, wup_ref[:, lo:hi], preferred_element_type=_F32)
            y = cb_ref[:, lo:hi]
            for tap in range(CONV_WIDTH):
                start = tap * batch
                y = y + up_ref[start:start + sub_rows, lo:hi] * cw_ref[tap:tap + 1, lo:hi]
            return y

        act = act_ref.at[sub % act_ref.shape[0]]
        for blk in range(D_FF // FF_BLOCK):
            lo, hi = blk * FF_BLOCK, (blk + 1) * FF_BLOCK
            a = conv(lo, hi)
            b = conv(D_FF + lo, D_FF + hi)
            act[:, lo:hi] = (a * jax.nn.sigmoid(a) * b).astype(_BF16)

        y = x + jnp.dot(act[...], wdown_ref[...], preferred_element_type=_F32)
        y = y * _rms_scale(y) * gf_ref[...]
        o_ref[:, t0:t0 + FFN_SUB_STEPS, :] = jnp.swapaxes(
            y.reshape(FFN_SUB_STEPS, batch, d), 0, 1)


def _const_spec(shape, grid_rank):
    zeros = (0,) * len(shape)
    index_map = (lambda j: zeros) if grid_rank == 1 else (lambda b, j: zeros)
    return pl.BlockSpec(shape, index_map, pipeline_mode=pl.Buffered(1))


def _decay_tables():
    log_gamma = jnp.log(1.0 - jnp.power(2.0, -5.0 - jnp.arange(RET_HEADS, dtype=_F32)))
    pos = jnp.arange(CHUNK, dtype=_F32)
    diff = pos[:, None] - pos[None, :]
    mask = jnp.where(diff >= 0.0,
                     jnp.exp(log_gamma[:, None, None] * jnp.maximum(diff, 0.0)[None]), 0.0)
    k_decay = jnp.exp(log_gamma[:, None] * (CHUNK - 1.0 - pos)[None])
    q_decay = jnp.exp(log_gamma[:, None] * (pos + 1.0)[None])
    chunk_decay = jnp.exp(log_gamma * CHUNK)
    widen = lambda t: jnp.repeat(t.T, HEAD_DIM, axis=1)
    return mask, widen(k_decay), widen(q_decay), jnp.repeat(chunk_decay, HEAD_DIM)[None, :]


def _rope_tables(seq):
    half = HEAD_DIM // 2
    inv_freq = jnp.power(ROPE_BASE, -jnp.arange(half, dtype=_F32) / half)
    ang = jnp.arange(seq, dtype=_F32)[:, None] * inv_freq[None, :]
    cos, sin = jnp.cos(ang), jnp.sin(ang)
    cos_full = jnp.concatenate([cos, cos], axis=-1)
    sin_signed = jnp.concatenate([-sin, sin], axis=-1)
    scale = HEAD_DIM ** -0.5
    return cos_full, sin_signed, cos_full * scale, sin_signed * scale


def kernel(x, mix_norm_g, w_in, ret_norm_g, sgu_ln_g, sgu_ln_b, sgu_w_s, sgu_b_s, w_out,
           ffn_norm_g, w_up, conv_w, conv_b, w_down, final_norm_g):
    batch, seq, d_model = x.shape
    assert w_in.shape[0] == 1
    assert d_model == D_MODEL and seq % MIX_ROWS == 0 and MIX_ROWS % CHUNK == 0
    assert batch % MIX_BATCH == 0 and seq % FFN_STEPS == 0 and FFN_STEPS % FFN_SUB_STEPS == 0
    assert D_FF % FF_BLOCK == 0
    l = 0

    cos, sin, cosk, sink = _rope_tables(seq)
    mask, kdec, qdec, cdec = _decay_tables()
    causal = jnp.tril(jnp.ones((CHUNK, CHUNK), dtype=_F32))
    out_shape = jax.ShapeDtypeStruct(x.shape, _F32)

    mix_x_spec = pl.BlockSpec((MIX_BATCH, MIX_ROWS, D_MODEL), lambda b, j: (b, j, 0))
    rope_spec = pl.BlockSpec((MIX_ROWS, HEAD_DIM), lambda b, j: (j, 0))
    c2 = lambda shape: _const_spec(shape, 2)
    x = pl.pallas_call(
        _mix_kernel,
        grid=(batch // MIX_BATCH, seq // MIX_ROWS),
        in_specs=[mix_x_spec, c2((1, D_MODEL)), c2(w_in.shape[1:]),
                  rope_spec, rope_spec, rope_spec, rope_spec,
                  c2(mask.shape), c2(kdec.shape), c2(qdec.shape), c2(cdec.shape),
                  c2((1, RET_WIDTH)), c2((1, SGU_WIDTH)), c2((1, SGU_WIDTH)),
                  c2(sgu_w_s.shape[1:]), c2((CHUNK, SGU_WIDTH)), c2(w_out.shape[1:])],
        out_specs=mix_x_spec,
        out_shape=out_shape,
        scratch_shapes=[pltpu.VMEM((MIX_BATCH, RET_HEADS, HEAD_DIM, HEAD_DIM), _F32),
                        pltpu.VMEM((MIX_BATCH, MIX_ROWS, RET_WIDTH + SGU_WIDTH), _BF16)],
        compiler_params=pltpu.CompilerParams(dimension_semantics=("arbitrary", "arbitrary"),
                                             vmem_limit_bytes=VMEM_LIMIT_BYTES),
        name="token_mix",
    )(x, mix_norm_g[l][None, :], w_in[l].astype(_BF16), cos, sin, cosk, sink,
      mask, kdec, qdec, cdec, ret_norm_g[l][None, :],
      sgu_ln_g[l].reshape(1, SGU_WIDTH), sgu_ln_b[l].reshape(1, SGU_WIDTH),
      (sgu_w_s[l] * causal[None]).astype(_BF16),
      jnp.repeat(sgu_b_s[l].T, GROUP_DIM, axis=1), w_out[l].astype(_BF16))

    sub_rows = batch * FFN_SUB_STEPS
    ffn_x_spec = pl.BlockSpec((batch, FFN_STEPS, D_MODEL), lambda j: (0, j, 0))
    c1 = lambda shape: _const_spec(shape, 1)
    return pl.pallas_call(
        _ffn_kernel,
        grid=(seq // FFN_STEPS,),
        in_specs=[ffn_x_spec, c1((1, D_MODEL)), c1(w_up.shape[1:]),
                  c1(conv_w.shape[1:]), c1((1, 2 * D_FF)),
                  c1(w_down.shape[1:]), c1((1, D_MODEL))],
        out_specs=ffn_x_spec,
        out_shape=out_shape,
        scratch_shapes=[pltpu.VMEM((sub_rows + (CONV_WIDTH - 1) * batch, 2 * D_FF), _F32),
                        pltpu.VMEM((2, sub_rows, D_FF), _BF16)],
        compiler_params=pltpu.CompilerParams(dimension_semantics=("arbitrary",),
                                             vmem_limit_bytes=VMEM_LIMIT_BYTES),
        name="channel_mix",
    )(x, ffn_norm_g[l][None, :], w_up[l].astype(_BF16), conv_w[l], conv_b[l][None, :],
      w_down[l].astype(_BF16), final_norm_g[None, :])
```

```python
import jax
import jax.numpy as jnp
import numpy as np
from jax import lax
from jax.experimental import pallas as pl
from jax.experimental.pallas import tpu as pltpu

D_MODEL = 1024
RET_HEADS = 4
HEAD_DIM = 128
RET_WIDTH = RET_HEADS * HEAD_DIM
SGU_GROUPS = 4
GROUP_DIM = 128
SGU_WIDTH = SGU_GROUPS * GROUP_DIM
CHUNK = 128
ROPE_BASE = 10000.0
D_FF = 2816
CONV_WIDTH = 3
EPS = 1e-6

MIX_ROWS = 256
MIX_BATCH = 4
FFN_STEPS = 64
FFN_SUB_STEPS = 32
FF_BLOCK = 256
VMEM_LIMIT_BYTES = 56 * 1024 * 1024

_F32 = jnp.float32
_BF16 = jnp.bfloat16
_SQRT_HALF = float(np.sqrt(0.5))


def _rms_scale(x):
    return lax.rsqrt(jnp.mean(x * x, axis=-1, keepdims=True) + EPS)


def _gelu(x):
    return 0.5 * x * (1.0 + lax.erf(x * _SQRT_HALF))


def _rotary(t, cos_full, sin_signed):
    return t * cos_full + pltpu.roll(t, HEAD_DIM // 2, axis=1) * sin_signed


def _mix_kernel(x_ref, g_ref, win_ref, cos_ref, sin_ref, cosk_ref, sink_ref,
                mask_ref, kdec_ref, qdec_ref, cdec_ref, retg_ref, lng_ref, lnb_ref,
                wsf_ref, bsr_ref, wout_ref, wup_ref, wdown_ref,
                o_ref, wup_bf_ref, wdown_bf_ref, state_ref, mix_ref, ws_ref, bs_ref):
    n_sub, rows, _ = x_ref.shape

    wup_bf_ref[...] = wup_ref[...].astype(_BF16)
    wdown_bf_ref[...] = wdown_ref[...].astype(_BF16)

    @pl.when(pl.program_id(1) == 0)
    def _():
        state_ref[...] = jnp.zeros_like(state_ref)

    @pl.when(jnp.logical_and(pl.program_id(0) == 0, pl.program_id(1) == 0))
    def _():
        row = lax.broadcasted_iota(jnp.int32, (CHUNK, CHUNK), 0)
        col = lax.broadcasted_iota(jnp.int32, (CHUNK, CHUNK), 1)
        causal = (row >= col).astype(_F32)
        for gr in range(SGU_GROUPS):
            ws_ref[gr] = (wsf_ref[gr] * causal).astype(_BF16)
            bias_row = jnp.broadcast_to(bsr_ref[gr:gr + 1, :], (CHUNK, CHUNK))
            bias_col = jnp.sum(jnp.where(row == col, bias_row, 0.0), axis=-1, keepdims=True)
            bs_ref[:, gr * GROUP_DIM:(gr + 1) * GROUP_DIM] = jnp.broadcast_to(
                bias_col, (CHUNK, GROUP_DIM))

    cos, sin = cos_ref[...], sin_ref[...]
    cosk, sink = cosk_ref[...], sink_ref[...]

    for i in range(n_sub):
        x = x_ref[i]
        h = (x * _rms_scale(x) * g_ref[...]).astype(_BF16)

        def proj(lo, hi):
            return jnp.dot(h, win_ref[:, lo:hi], preferred_element_type=_F32)

        q = proj(0, RET_WIDTH)
        k = proj(RET_WIDTH, 2 * RET_WIDTH)
        v = proj(2 * RET_WIDTH, 3 * RET_WIDTH)
        g = proj(3 * RET_WIDTH, 4 * RET_WIDTH)
        u = proj(4 * RET_WIDTH, 4 * RET_WIDTH + SGU_WIDTH)
        sv = proj(4 * RET_WIDTH + SGU_WIDTH, 4 * RET_WIDTH + 2 * SGU_WIDTH)

        for c in range(rows // CHUNK):
            r = slice(c * CHUNK, (c + 1) * CHUNK)
            for hd in range(RET_HEADS):
                l = slice(hd * HEAD_DIM, (hd + 1) * HEAD_DIM)
                qh = _rotary(q[r, l], cos[r], sin[r])
                kh = _rotary(k[r, l], cosk[r], sink[r])
                vh = v[r, l].astype(_BF16)
                scores = lax.dot_general(qh.astype(_BF16), kh.astype(_BF16),
                                         (((1,), (1,)), ((), ())),
                                         preferred_element_type=_F32) * mask_ref[hd]
                intra = jnp.dot(scores.astype(_BF16), vh, preferred_element_type=_F32)
                state = state_ref[i, hd]
                cross = jnp.dot((qh * qdec_ref[:, l]).astype(_BF16), state.astype(_BF16),
                                preferred_element_type=_F32)
                kv = lax.dot_general((kh * kdec_ref[:, l]).astype(_BF16), vh,
                                     (((0,), (0,)), ((), ())), preferred_element_type=_F32)
                state_ref[i, hd] = state * cdec_ref[:, l] + kv
                ret = intra + cross
                ret = ret * _rms_scale(ret) * retg_ref[:, l]
                gh = g[r, l]
                mix_ref[i, r, l] = (gh * jax.nn.sigmoid(gh) * ret).astype(_BF16)
            for gr in range(SGU_GROUPS):
                l = slice(gr * GROUP_DIM, (gr + 1) * GROUP_DIM)
                gv = _gelu(sv[r, l])
                xc = gv - jnp.mean(gv, axis=-1, keepdims=True)
                vn = xc * _rms_scale(xc) * lng_ref[:, l] + lnb_ref[:, l]
                mixed = jnp.dot(ws_ref[gr], vn.astype(_BF16), preferred_element_type=_F32)
                out = _gelu(u[r, l]) * (mixed + bs_ref[:, l])
                mix_ref[i, r, RET_WIDTH + gr * GROUP_DIM:RET_WIDTH + (gr + 1) * GROUP_DIM] = (
                    out.astype(_BF16))

        o_ref[i] = x + jnp.dot(mix_ref[i], wout_ref[...], preferred_element_type=_F32)


def _ffn_kernel(x_ref, g_ref, wup_ref, cw_ref, cb_ref, wdown_ref, gf_ref, o_ref,
                up_ref, act_ref):
    batch, steps, d = x_ref.shape
    sub_rows = batch * FFN_SUB_STEPS
    carry = (CONV_WIDTH - 1) * batch

    @pl.when(pl.program_id(0) == 0)
    def _():
        up_ref[sub_rows:sub_rows + carry, :] = jnp.zeros((carry, up_ref.shape[1]), _F32)

    for sub in range(steps // FFN_SUB_STEPS):
        t0 = sub * FFN_SUB_STEPS
        x = jnp.swapaxes(x_ref[:, t0:t0 + FFN_SUB_STEPS, :], 0, 1).reshape(sub_rows, d)
        h = (x * _rms_scale(x) * g_ref[...]).astype(_BF16)
        up_ref[0:carry, :] = up_ref[sub_rows:sub_rows + carry, :]

        def conv(lo, hi):
            up_ref[carry:carry + sub_rows, lo:hi] = jnp.dot(
                h, wup_ref[:, lo:hi], preferred_element_type=_F32)
            y = cb_ref[:, lo:hi]
            for tap in range(CONV_WIDTH):
                start = tap * batch
                y = y + up_ref[start:start + sub_rows, lo:hi] * cw_ref[tap:tap + 1, lo:hi]
            return y

        act = act_ref.at[sub % act_ref.shape[0]]
        for blk in range(D_FF // FF_BLOCK):
            lo, hi = blk * FF_BLOCK, (blk + 1) * FF_BLOCK
            a = conv(lo, hi)
            b = conv(D_FF + lo, D_FF + hi)
            act[:, lo:hi] = (a * jax.nn.sigmoid(a) * b).astype(_BF16)

        y = x + jnp.dot(act[...], wdown_ref[...], preferred_element_type=_F32)
        y = y * _rms_scale(y) * gf_ref[...]
        o_ref[:, t0:t0 + FFN_SUB_STEPS, :] = jnp.swapaxes(
            y.reshape(FFN_SUB_STEPS, batch, d), 0, 1)


def _const_spec(shape, grid_rank):
    zeros = (0,) * len(shape)
    index_map = (lambda j: zeros) if grid_rank == 1 else (lambda b, j: zeros)
    return pl.BlockSpec(shape, index_map, pipeline_mode=pl.Buffered(1))


def _as_f32_constants(tables):
    return tuple(jnp.asarray(t.astype(np.float32)) for t in tables)


def _decay_tables():
    log_gamma = np.log(1.0 - np.power(2.0, -5.0 - np.arange(RET_HEADS, dtype=np.float64)))
    pos = np.arange(CHUNK, dtype=np.float64)
    diff = pos[:, None] - pos[None, :]
    mask = np.where(diff >= 0.0,
                    np.exp(log_gamma[:, None, None] * np.maximum(diff, 0.0)[None]), 0.0)
    k_decay = np.exp(log_gamma[:, None] * (CHUNK - 1.0 - pos)[None])
    q_decay = np.exp(log_gamma[:, None] * (pos + 1.0)[None])
    chunk_decay = np.exp(log_gamma * CHUNK)
    widen = lambda t: np.repeat(t.T, HEAD_DIM, axis=1)
    return _as_f32_constants(
        (mask, widen(k_decay), widen(q_decay), np.repeat(chunk_decay, HEAD_DIM)[None, :]))


def _rope_tables(seq):
    half = HEAD_DIM // 2
    inv_freq = np.power(ROPE_BASE, -np.arange(half, dtype=np.float64) / half)
    ang = np.arange(seq, dtype=np.float64)[:, None] * inv_freq[None, :]
    cos, sin = np.cos(ang), np.sin(ang)
    cos_full = np.concatenate([cos, cos], axis=-1)
    sin_signed = np.concatenate([-sin, sin], axis=-1)
    scale = HEAD_DIM ** -0.5
    return _as_f32_constants((cos_full, sin_signed, cos_full * scale, sin_signed * scale))


def kernel(x, mix_norm_g, w_in, ret_norm_g, sgu_ln_g, sgu_ln_b, sgu_w_s, sgu_b_s, w_out,
           ffn_norm_g, w_up, conv_w, conv_b, w_down, final_norm_g):
    batch, seq, d_model = x.shape
    assert w_in.shape[0] == 1
    assert d_model == D_MODEL and seq % MIX_ROWS == 0 and MIX_ROWS % CHUNK == 0
    assert batch % MIX_BATCH == 0 and seq % FFN_STEPS == 0 and FFN_STEPS % FFN_SUB_STEPS == 0
    assert D_FF % FF_BLOCK == 0
    l = 0

    cos, sin, cosk, sink = _rope_tables(seq)
    mask, kdec, qdec, cdec = _decay_tables()
    out_shape = jax.ShapeDtypeStruct(x.shape, _F32)

    mix_grid = (batch // MIX_BATCH, seq // MIX_ROWS)
    mix_steps = mix_grid[0] * mix_grid[1]
    assert D_MODEL % (16 * mix_steps) == 0 and D_FF % (16 * mix_steps) == 0
    mix_x_spec = pl.BlockSpec((MIX_BATCH, MIX_ROWS, D_MODEL), lambda b, j: (b, j, 0))
    rope_spec = pl.BlockSpec((MIX_ROWS, HEAD_DIM), lambda b, j: (j, 0))
    wup_spec = pl.BlockSpec((D_MODEL // mix_steps, 2 * D_FF),
                            lambda b, j: (b * mix_grid[1] + j, 0))
    wdown_spec = pl.BlockSpec((D_FF // mix_steps, D_MODEL),
                              lambda b, j: (b * mix_grid[1] + j, 0))
    c2 = lambda shape: _const_spec(shape, 2)
    x, w_up_bf, w_down_bf = pl.pallas_call(
        _mix_kernel,
        grid=mix_grid,
        in_specs=[mix_x_spec, c2((1, D_MODEL)), c2(w_in.shape[1:]),
                  rope_spec, rope_spec, rope_spec, rope_spec,
                  c2(mask.shape), c2(kdec.shape), c2(qdec.shape), c2(cdec.shape),
                  c2((1, RET_WIDTH)), c2((1, SGU_WIDTH)), c2((1, SGU_WIDTH)),
                  c2(sgu_w_s.shape[1:]), c2(sgu_b_s.shape[1:]), c2(w_out.shape[1:]),
                  wup_spec, wdown_spec],
        out_specs=[mix_x_spec, wup_spec, wdown_spec],
        out_shape=[out_shape, jax.ShapeDtypeStruct(w_up.shape[1:], _BF16),
                   jax.ShapeDtypeStruct(w_down.shape[1:], _BF16)],
        scratch_shapes=[pltpu.VMEM((MIX_BATCH, RET_HEADS, HEAD_DIM, HEAD_DIM), _F32),
                        pltpu.VMEM((MIX_BATCH, MIX_ROWS, RET_WIDTH + SGU_WIDTH), _BF16),
                        pltpu.VMEM((SGU_GROUPS, CHUNK, CHUNK), _BF16),
                        pltpu.VMEM((CHUNK, SGU_WIDTH), _F32)],
        compiler_params=pltpu.CompilerParams(dimension_semantics=("arbitrary", "arbitrary"),
                                             vmem_limit_bytes=VMEM_LIMIT_BYTES),
        name="token_mix",
    )(x, mix_norm_g[l][None, :], w_in[l].astype(_BF16), cos, sin, cosk, sink,
      mask, kdec, qdec, cdec, ret_norm_g[l][None, :],
      sgu_ln_g[l].reshape(1, SGU_WIDTH), sgu_ln_b[l].reshape(1, SGU_WIDTH),
      sgu_w_s[l], sgu_b_s[l], w_out[l].astype(_BF16), w_up[l], w_down[l])

    sub_rows = batch * FFN_SUB_STEPS
    ffn_x_spec = pl.BlockSpec((batch, FFN_STEPS, D_MODEL), lambda j: (0, j, 0))
    c1 = lambda shape: _const_spec(shape, 1)
    return pl.pallas_call(
        _ffn_kernel,
        grid=(seq // FFN_STEPS,),
        in_specs=[ffn_x_spec, c1((1, D_MODEL)), c1(w_up.shape[1:]),
                  c1(conv_w.shape[1:]), c1((1, 2 * D_FF)),
                  c1(w_down.shape[1:]), c1((1, D_MODEL))],
        out_specs=ffn_x_spec,
        out_shape=out_shape,
        scratch_shapes=[pltpu.VMEM((sub_rows + (CONV_WIDTH - 1) * batch, 2 * D_FF), _F32),
                        pltpu.VMEM((2, sub_rows, D_FF), _BF16)],
        compiler_params=pltpu.CompilerParams(dimension_semantics=("arbitrary",),
                                             vmem_limit_bytes=VMEM_LIMIT_BYTES),
        name="channel_mix",
    )(x, ffn_norm_g[l][None, :], w_up_bf, conv_w[l], conv_b[l][None, :],
      w_down_bf, final_norm_g[None, :])
```

```python
import jax
import jax.numpy as jnp
import numpy as np
from jax import lax
from jax.experimental import pallas as pl
from jax.experimental.pallas import tpu as pltpu

D_MODEL = 1024
RET_HEADS = 4
HEAD_DIM = 128
RET_WIDTH = RET_HEADS * HEAD_DIM
SGU_GROUPS = 4
GROUP_DIM = 128
SGU_WIDTH = SGU_GROUPS * GROUP_DIM
CHUNK = 128
ROPE_BASE = 10000.0
D_FF = 2816
CONV_WIDTH = 3
EPS = 1e-6

MIX_ROWS = 256
MIX_BATCH = 4
FFN_STEPS = 64
FFN_SUB_STEPS = 32
FF_BLOCK = 256
VMEM_LIMIT_BYTES = 56 * 1024 * 1024

BF16_ROWS = 16

_F32 = jnp.float32
_BF16 = jnp.bfloat16
_SQRT_HALF = float(np.sqrt(0.5))


def _rms_scale(x):
    return lax.rsqrt(jnp.mean(x * x, axis=-1, keepdims=True) + EPS)


def _gelu(x):
    return 0.5 * x * (1.0 + lax.erf(x * _SQRT_HALF))


def _rotary(t, cos_full, sin_signed):
    return t * cos_full + pltpu.roll(t, HEAD_DIM // 2, axis=1) * sin_signed


def _mix_kernel(x_ref, g_ref, win_ref, cos_ref, sin_ref, cosk_ref, sink_ref,
                mask_ref, kdec_ref, qdec_ref, cdec_ref, retg_ref, lng_ref, lnb_ref,
                wsf_ref, bsr_ref, wout_ref, wup_ref, wdown_ref,
                o_ref, wup_bf_ref, wdown_bf_ref, state_ref, mix_ref, ws_ref, bs_ref,
                proj_ref):
    n_sub, rows, _ = x_ref.shape

    def cast_up(lo):
        wup_bf_ref[lo:lo + BF16_ROWS, :] = wup_ref[lo:lo + BF16_ROWS, :].astype(_BF16)

    def cast_down():
        wdown_bf_ref[...] = wdown_ref[...].astype(_BF16)

    cast_units = [lambda lo=lo: cast_up(lo) for lo in range(0, wup_ref.shape[0], BF16_ROWS)]
    cast_units.append(cast_down)

    @pl.when(pl.program_id(1) == 0)
    def _():
        state_ref[...] = jnp.zeros_like(state_ref)

    @pl.when(jnp.logical_and(pl.program_id(0) == 0, pl.program_id(1) == 0))
    def _():
        row = lax.broadcasted_iota(jnp.int32, (CHUNK, CHUNK), 0)
        col = lax.broadcasted_iota(jnp.int32, (CHUNK, CHUNK), 1)
        causal = (row >= col).astype(_F32)
        for gr in range(SGU_GROUPS):
            ws_ref[gr] = (wsf_ref[gr] * causal).astype(_BF16)
            bias_row = jnp.broadcast_to(bsr_ref[gr:gr + 1, :], (CHUNK, CHUNK))
            bias_col = jnp.sum(jnp.where(row == col, bias_row, 0.0), axis=-1, keepdims=True)
            bs_ref[:, gr * GROUP_DIM:(gr + 1) * GROUP_DIM] = jnp.broadcast_to(
                bias_col, (CHUNK, GROUP_DIM))

    cos, sin = cos_ref[...], sin_ref[...]
    cosk, sink = cosk_ref[...], sink_ref[...]

    def project(i):
        pb = proj_ref.at[i % proj_ref.shape[0]]
        cache = {}

        def unit(lo):
            if not cache:
                x = x_ref[i]
                cache["h"] = (x * _rms_scale(x) * g_ref[...]).astype(_BF16)
            pb[:, lo:lo + RET_WIDTH] = jnp.dot(cache["h"], win_ref[:, lo:lo + RET_WIDTH],
                                               preferred_element_type=_F32)

        return [lambda lo=lo: unit(lo) for lo in range(0, win_ref.shape[1], RET_WIDTH)]

    def mix(i):
        pb = proj_ref.at[i % proj_ref.shape[0]]
        q = pb.at[:, 0:RET_WIDTH]
        k = pb.at[:, RET_WIDTH:2 * RET_WIDTH]
        v = pb.at[:, 2 * RET_WIDTH:3 * RET_WIDTH]
        g = pb.at[:, 3 * RET_WIDTH:4 * RET_WIDTH]
        u = pb.at[:, 4 * RET_WIDTH:4 * RET_WIDTH + SGU_WIDTH]
        sv = pb.at[:, 4 * RET_WIDTH + SGU_WIDTH:4 * RET_WIDTH + 2 * SGU_WIDTH]

        def head(c, hd):
            r = slice(c * CHUNK, (c + 1) * CHUNK)
            l = slice(hd * HEAD_DIM, (hd + 1) * HEAD_DIM)
            qh = _rotary(q[r, l], cos[r], sin[r])
            kh = _rotary(k[r, l], cosk[r], sink[r])
            vh = v[r, l].astype(_BF16)
            scores = lax.dot_general(qh.astype(_BF16), kh.astype(_BF16),
                                     (((1,), (1,)), ((), ())),
                                     preferred_element_type=_F32) * mask_ref[hd]
            intra = jnp.dot(scores.astype(_BF16), vh, preferred_element_type=_F32)
            state = state_ref[i, hd]
            cross = jnp.dot((qh * qdec_ref[:, l]).astype(_BF16), state.astype(_BF16),
                            preferred_element_type=_F32)
            kv = lax.dot_general((kh * kdec_ref[:, l]).astype(_BF16), vh,
                                 (((0,), (0,)), ((), ())), preferred_element_type=_F32)
            state_ref[i, hd] = state * cdec_ref[:, l] + kv
            ret = intra + cross
            ret = ret * _rms_scale(ret) * retg_ref[:, l]
            gh = g[r, l]
            mix_ref[i, r, l] = (gh * jax.nn.sigmoid(gh) * ret).astype(_BF16)

        def group(c, gr):
            r = slice(c * CHUNK, (c + 1) * CHUNK)
            l = slice(gr * GROUP_DIM, (gr + 1) * GROUP_DIM)
            gv = _gelu(sv[r, l])
            xc = gv - jnp.mean(gv, axis=-1, keepdims=True)
            vn = xc * _rms_scale(xc) * lng_ref[gr:gr + 1, :] + lnb_ref[gr:gr + 1, :]
            mixed = jnp.dot(ws_ref[gr], vn.astype(_BF16), preferred_element_type=_F32)
            out = _gelu(u[r, l]) * (mixed + bs_ref[:, l])
            mix_ref[i, r, RET_WIDTH + gr * GROUP_DIM:RET_WIDTH + (gr + 1) * GROUP_DIM] = (
                out.astype(_BF16))

        def out_proj():
            o_ref[i] = x_ref[i] + jnp.dot(mix_ref[i], wout_ref[...],
                                          preferred_element_type=_F32)

        units = []
        for c in range(rows // CHUNK):
            units += [lambda c=c, hd=hd: head(c, hd) for hd in range(RET_HEADS)]
            units += [lambda c=c, gr=gr: group(c, gr) for gr in range(SGU_GROUPS)]
        return units + [out_proj]

    for unit in project(0):
        unit()
    for i in range(n_sub):
        mix_units = mix(i)
        proj_units = project(i + 1) if i + 1 < n_sub else cast_units
        stride = -(-len(mix_units) // (len(proj_units) + 1))
        for n, unit in enumerate(mix_units):
            unit()
            if proj_units and (n + 1) % stride == 0:
                proj_units.pop(0)()
        for unit in proj_units:
            unit()


def _ffn_kernel(x_ref, g_ref, wup_ref, cw_ref, cb_ref, wdown_ref, gf_ref, o_ref,
                up_ref, act_ref):
    batch, steps, d = x_ref.shape
    sub_rows = batch * FFN_SUB_STEPS
    carry = (CONV_WIDTH - 1) * batch

    @pl.when(pl.program_id(0) == 0)
    def _():
        up_ref[sub_rows:sub_rows + carry, :] = jnp.zeros((carry, up_ref.shape[1]), _F32)

    for sub in range(steps // FFN_SUB_STEPS):
        t0 = sub * FFN_SUB_STEPS
        x = jnp.swapaxes(x_ref[:, t0:t0 + FFN_SUB_STEPS, :], 0, 1).reshape(sub_rows, d)
        h = (x * _rms_scale(x) * g_ref[...]).astype(_BF16)
        up_ref[0:carry, :] = up_ref[sub_rows:sub_rows + carry, :]

        def conv(lo, hi):
            up_ref[carry:carry + sub_rows, lo:hi] = jnp.dot(
                h, wup_ref[:, lo:hi], preferred_element_type=_F32)
            y = cb_ref[:, lo:hi]
            for tap in range(CONV_WIDTH):
                start = tap * batch
                y = y + up_ref[start:start + sub_rows, lo:hi] * cw_ref[tap:tap + 1, lo:hi]
            return y

        act = act_ref.at[sub % act_ref.shape[0]]
        for blk in range(D_FF // FF_BLOCK):
            lo, hi = blk * FF_BLOCK, (blk + 1) * FF_BLOCK
            a = conv(lo, hi)
            b = conv(D_FF + lo, D_FF + hi)
            act[:, lo:hi] = (a * jax.nn.sigmoid(a) * b).astype(_BF16)

        y = x + jnp.dot(act[...], wdown_ref[...], preferred_element_type=_F32)
        y = y * _rms_scale(y) * gf_ref[...]
        o_ref[:, t0:t0 + FFN_SUB_STEPS, :] = jnp.swapaxes(
            y.reshape(FFN_SUB_STEPS, batch, d), 0, 1)


def _const_spec(shape, grid_rank):
    zeros = (0,) * len(shape)
    index_map = (lambda j: zeros) if grid_rank == 1 else (lambda b, j: zeros)
    return pl.BlockSpec(shape, index_map, pipeline_mode=pl.Buffered(1))


def _as_f32_constants(tables):
    return tuple(jnp.asarray(t.astype(np.float32)) for t in tables)


def _decay_tables():
    log_gamma = np.log(1.0 - np.power(2.0, -5.0 - np.arange(RET_HEADS, dtype=np.float64)))
    pos = np.arange(CHUNK, dtype=np.float64)
    diff = pos[:, None] - pos[None, :]
    mask = np.where(diff >= 0.0,
                    np.exp(log_gamma[:, None, None] * np.maximum(diff, 0.0)[None]), 0.0)
    k_decay = np.exp(log_gamma[:, None] * (CHUNK - 1.0 - pos)[None])
    q_decay = np.exp(log_gamma[:, None] * (pos + 1.0)[None])
    chunk_decay = np.exp(log_gamma * CHUNK)
    widen = lambda t: np.repeat(t.T, HEAD_DIM, axis=1)
    return _as_f32_constants(
        (mask, widen(k_decay), widen(q_decay), np.repeat(chunk_decay, HEAD_DIM)[None, :]))


def _rope_tables(seq):
    half = HEAD_DIM // 2
    inv_freq = np.power(ROPE_BASE, -np.arange(half, dtype=np.float64) / half)
    ang = np.arange(seq, dtype=np.float64)[:, None] * inv_freq[None, :]
    cos, sin = np.cos(ang), np.sin(ang)
    cos_full = np.concatenate([cos, cos], axis=-1)
    sin_signed = np.concatenate([-sin, sin], axis=-1)
    scale = HEAD_DIM ** -0.5
    return _as_f32_constants((cos_full, sin_signed, cos_full * scale, sin_signed * scale))


def kernel(x, mix_norm_g, w_in, ret_norm_g, sgu_ln_g, sgu_ln_b, sgu_w_s, sgu_b_s, w_out,
           ffn_norm_g, w_up, conv_w, conv_b, w_down, final_norm_g):
    batch, seq, d_model = x.shape
    assert w_in.shape[0] == 1
    assert d_model == D_MODEL and seq % MIX_ROWS == 0 and MIX_ROWS % CHUNK == 0
    assert batch % MIX_BATCH == 0 and seq % FFN_STEPS == 0 and FFN_STEPS % FFN_SUB_STEPS == 0
    assert D_FF % FF_BLOCK == 0
    l = 0

    cos, sin, cosk, sink = _rope_tables(seq)
    mask, kdec, qdec, cdec = _decay_tables()
    out_shape = jax.ShapeDtypeStruct(x.shape, _F32)

    mix_grid = (batch // MIX_BATCH, seq // MIX_ROWS)
    mix_steps = mix_grid[0] * mix_grid[1]
    assert D_MODEL % (16 * mix_steps) == 0 and D_FF % (16 * mix_steps) == 0
    mix_x_spec = pl.BlockSpec((MIX_BATCH, MIX_ROWS, D_MODEL), lambda b, j: (b, j, 0))
    rope_spec = pl.BlockSpec((MIX_ROWS, HEAD_DIM), lambda b, j: (j, 0))
    wup_spec = pl.BlockSpec((D_MODEL // mix_steps, 2 * D_FF),
                            lambda b, j: (b * mix_grid[1] + j, 0))
    wdown_spec = pl.BlockSpec((D_FF // mix_steps, D_MODEL),
                              lambda b, j: (b * mix_grid[1] + j, 0))
    c2 = lambda shape: _const_spec(shape, 2)
    x, w_up_bf, w_down_bf = pl.pallas_call(
        _mix_kernel,
        grid=mix_grid,
        in_specs=[mix_x_spec, c2((1, D_MODEL)), c2(w_in.shape[1:]),
                  rope_spec, rope_spec, rope_spec, rope_spec,
                  c2(mask.shape), c2(kdec.shape), c2(qdec.shape), c2(cdec.shape),
                  c2((1, RET_WIDTH)), c2(sgu_ln_g.shape[1:]), c2(sgu_ln_b.shape[1:]),
                  c2(sgu_w_s.shape[1:]), c2(sgu_b_s.shape[1:]), c2(w_out.shape[1:]),
                  wup_spec, wdown_spec],
        out_specs=[mix_x_spec, wup_spec, wdown_spec],
        out_shape=[out_shape, jax.ShapeDtypeStruct(w_up.shape[1:], _BF16),
                   jax.ShapeDtypeStruct(w_down.shape[1:], _BF16)],
        scratch_shapes=[pltpu.VMEM((MIX_BATCH, RET_HEADS, HEAD_DIM, HEAD_DIM), _F32),
                        pltpu.VMEM((MIX_BATCH, MIX_ROWS, RET_WIDTH + SGU_WIDTH), _BF16),
                        pltpu.VMEM((SGU_GROUPS, CHUNK, CHUNK), _BF16),
                        pltpu.VMEM((CHUNK, SGU_WIDTH), _F32),
                        pltpu.VMEM((2, MIX_ROWS, w_in.shape[2]), _F32)],
        compiler_params=pltpu.CompilerParams(dimension_semantics=("arbitrary", "arbitrary"),
                                             vmem_limit_bytes=VMEM_LIMIT_BYTES),
        name="token_mix",
    )(x, mix_norm_g[l][None, :], w_in[l].astype(_BF16), cos, sin, cosk, sink,
      mask, kdec, qdec, cdec, ret_norm_g[l][None, :],
      sgu_ln_g[l], sgu_ln_b[l],
      sgu_w_s[l], sgu_b_s[l], w_out[l].astype(_BF16), w_up[l], w_down[l])

    sub_rows = batch * FFN_SUB_STEPS
    ffn_x_spec = pl.BlockSpec((batch, FFN_STEPS, D_MODEL), lambda j: (0, j, 0))
    c1 = lambda shape: _const_spec(shape, 1)
    return pl.pallas_call(
        _ffn_kernel,
        grid=(seq // FFN_STEPS,),
        in_specs=[ffn_x_spec, c1((1, D_MODEL)), c1(w_up.shape[1:]),
                  c1(conv_w.shape[1:]), c1((1, 2 * D_FF)),
                  c1(w_down.shape[1:]), c1((1, D_MODEL))],
        out_specs=ffn_x_spec,
        out_shape=out_shape,
        scratch_shapes=[pltpu.VMEM((sub_rows + (CONV_WIDTH - 1) * batch, 2 * D_FF), _F32),
                        pltpu.VMEM((2, sub_rows, D_FF), _BF16)],
        compiler_params=pltpu.CompilerParams(dimension_semantics=("arbitrary",),
                                             vmem_limit_bytes=VMEM_LIMIT_BYTES),
        name="channel_mix",
    )(x, ffn_norm_g[l][None, :], w_up_bf, conv_w[l], conv_b[l][None, :],
      w_down_bf, final_norm_g[None, :])
```

```python
import jax
import jax.numpy as jnp
import numpy as np
from jax import lax
from jax.experimental import pallas as pl
from jax.experimental.pallas import tpu as pltpu

D_MODEL = 1024
RET_HEADS = 4
HEAD_DIM = 128
RET_WIDTH = RET_HEADS * HEAD_DIM
SGU_GROUPS = 4
GROUP_DIM = 128
SGU_WIDTH = SGU_GROUPS * GROUP_DIM
CHUNK = 128
ROPE_BASE = 10000.0
D_FF = 2816
CONV_WIDTH = 3
EPS = 1e-6

MIX_ROWS = 256
MIX_BATCH = 4
FFN_STEPS = 64
FFN_SUB_STEPS = 32
FF_BLOCK = 256
VMEM_LIMIT_BYTES = 56 * 1024 * 1024

CAST_ROWS = 128
OUT_PROJ_DELAY = 2
BF16_ROWS = 16

_F32 = jnp.float32
_BF16 = jnp.bfloat16
_SQRT_HALF = float(np.sqrt(0.5))


def _rms_scale(x):
    return lax.rsqrt(jnp.mean(x * x, axis=-1, keepdims=True) + EPS)


def _gelu(x):
    return 0.5 * x * (1.0 + lax.erf(x * _SQRT_HALF))


def _rotary(t, cos_full, sin_signed):
    return t * cos_full + pltpu.roll(t, HEAD_DIM // 2, axis=1) * sin_signed


def _cast_rows_to_bf16(src_hbm, dst_ref, stage_ref, sems):
    rows = stage_ref.shape[1]
    n_chunks = src_hbm.shape[0] // rows

    def chunk_copy(c):
        return pltpu.make_async_copy(src_hbm.at[pl.ds(c * rows, rows), :],
                                     stage_ref.at[c % 2], sems.at[c % 2])

    chunk_copy(0).start()
    for c in range(n_chunks):
        if c + 1 < n_chunks:
            chunk_copy(c + 1).start()
        chunk_copy(c).wait()
        dst_ref[c * rows:(c + 1) * rows, :] = stage_ref[c % 2].astype(_BF16)


def _mix_kernel(x_ref, g_ref, win_hbm, cos_ref, sin_ref, cosk_ref, sink_ref,
                mask_ref, kdec_ref, qdec_ref, cdec_ref, retg_ref, lng_ref, lnb_ref,
                wsf_ref, bsr_ref, wout_hbm, wup_ref, wdown_ref,
                o_ref, wup_bf_ref, wdown_bf_ref, state_ref, mix_ref, ws_ref, bs_ref,
                proj_ref, win_ref, wout_ref, win_stage, wout_stage, win_sems, wout_sems):
    n_sub, rows, _ = x_ref.shape

    def cast_up(lo):
        wup_bf_ref[lo:lo + BF16_ROWS, :] = wup_ref[lo:lo + BF16_ROWS, :].astype(_BF16)

    def cast_down():
        wdown_bf_ref[...] = wdown_ref[...].astype(_BF16)

    cast_units = [lambda lo=lo: cast_up(lo) for lo in range(0, wup_ref.shape[0], BF16_ROWS)]
    cast_units.append(cast_down)

    @pl.when(pl.program_id(1) == 0)
    def _():
        state_ref[...] = jnp.zeros_like(state_ref)

    @pl.when(jnp.logical_and(pl.program_id(0) == 0, pl.program_id(1) == 0))
    def _():
        _cast_rows_to_bf16(win_hbm, win_ref, win_stage, win_sems)
        _cast_rows_to_bf16(wout_hbm, wout_ref, wout_stage, wout_sems)
        row = lax.broadcasted_iota(jnp.int32, (CHUNK, CHUNK), 0)
        col = lax.broadcasted_iota(jnp.int32, (CHUNK, CHUNK), 1)
        causal = (row >= col).astype(_F32)
        for gr in range(SGU_GROUPS):
            ws_ref[gr] = (wsf_ref[gr] * causal).astype(_BF16)
            bias_row = jnp.broadcast_to(bsr_ref[gr:gr + 1, :], (CHUNK, CHUNK))
            bias_col = jnp.sum(jnp.where(row == col, bias_row, 0.0), axis=-1, keepdims=True)
            bs_ref[:, gr * GROUP_DIM:(gr + 1) * GROUP_DIM] = jnp.broadcast_to(
                bias_col, (CHUNK, GROUP_DIM))

    cos, sin = cos_ref[...], sin_ref[...]
    cosk, sink = cosk_ref[...], sink_ref[...]

    def project(i):
        pb = proj_ref.at[i % proj_ref.shape[0]]
        cache = {}

        def unit(lo):
            if not cache:
                x = x_ref[i]
                cache["h"] = (x * _rms_scale(x) * g_ref[...]).astype(_BF16)
            pb[:, lo:lo + RET_WIDTH] = jnp.dot(cache["h"], win_ref[:, lo:lo + RET_WIDTH],
                                               preferred_element_type=_F32)

        return [lambda lo=lo: unit(lo) for lo in range(0, win_ref.shape[1], RET_WIDTH)]

    def mix(i):
        pb = proj_ref.at[i % proj_ref.shape[0]]
        q = pb.at[:, 0:RET_WIDTH]
        k = pb.at[:, RET_WIDTH:2 * RET_WIDTH]
        v = pb.at[:, 2 * RET_WIDTH:3 * RET_WIDTH]
        g = pb.at[:, 3 * RET_WIDTH:4 * RET_WIDTH]
        u = pb.at[:, 4 * RET_WIDTH:4 * RET_WIDTH + SGU_WIDTH]
        sv = pb.at[:, 4 * RET_WIDTH + SGU_WIDTH:4 * RET_WIDTH + 2 * SGU_WIDTH]

        def head(c, hd):
            r = slice(c * CHUNK, (c + 1) * CHUNK)
            l = slice(hd * HEAD_DIM, (hd + 1) * HEAD_DIM)
            qh = _rotary(q[r, l], cos[r], sin[r])
            kh = _rotary(k[r, l], cosk[r], sink[r])
            vh = v[r, l].astype(_BF16)
            scores = lax.dot_general(qh.astype(_BF16), kh.astype(_BF16),
                                     (((1,), (1,)), ((), ())),
                                     preferred_element_type=_F32) * mask_ref[hd]
            intra = jnp.dot(scores.astype(_BF16), vh, preferred_element_type=_F32)
            state = state_ref[i, hd]
            cross = jnp.dot((qh * qdec_ref[:, l]).astype(_BF16), state.astype(_BF16),
                            preferred_element_type=_F32)
            kv = lax.dot_general((kh * kdec_ref[:, l]).astype(_BF16), vh,
                                 (((0,), (0,)), ((), ())), preferred_element_type=_F32)
            state_ref[i, hd] = state * cdec_ref[:, l] + kv
            ret = intra + cross
            ret = ret * _rms_scale(ret) * retg_ref[:, l]
            gh = g[r, l]
            mix_ref[i, r, l] = (gh * jax.nn.sigmoid(gh) * ret).astype(_BF16)

        def group(c, gr):
            r = slice(c * CHUNK, (c + 1) * CHUNK)
            l = slice(gr * GROUP_DIM, (gr + 1) * GROUP_DIM)
            gv = _gelu(sv[r, l])
            xc = gv - jnp.mean(gv, axis=-1, keepdims=True)
            vn = xc * _rms_scale(xc) * lng_ref[gr:gr + 1, :] + lnb_ref[gr:gr + 1, :]
            mixed = jnp.dot(ws_ref[gr], vn.astype(_BF16), preferred_element_type=_F32)
            out = _gelu(u[r, l]) * (mixed + bs_ref[:, l])
            mix_ref[i, r, RET_WIDTH + gr * GROUP_DIM:RET_WIDTH + (gr + 1) * GROUP_DIM] = (
                out.astype(_BF16))

        def out_proj():
            o_ref[i] = x_ref[i] + jnp.dot(mix_ref[i], wout_ref[...],
                                          preferred_element_type=_F32)

        units = []
        for c in range(rows // CHUNK):
            units += [lambda c=c, hd=hd: head(c, hd) for hd in range(RET_HEADS)]
            units += [lambda c=c, gr=gr: group(c, gr) for gr in range(SGU_GROUPS)]
        return units + [out_proj]

    for unit in project(0):
        unit()
    pending_out = None
    for i in range(n_sub):
        *mix_units, out_proj = mix(i)
        if pending_out is not None:
            mix_units.insert(OUT_PROJ_DELAY, pending_out)
        pending_out = out_proj
        if i + 1 == n_sub:
            mix_units.append(pending_out)
        proj_units = project(i + 1) if i + 1 < n_sub else cast_units
        stride = -(-len(mix_units) // (len(proj_units) + 1))
        for n, unit in enumerate(mix_units):
            unit()
            if proj_units and (n + 1) % stride == 0:
                proj_units.pop(0)()
        for unit in proj_units:
            unit()


def _ffn_kernel(x_ref, g_ref, wup_ref, cw_ref, cb_ref, wdown_ref, gf_ref, o_ref,
                up_ref, act_ref):
    batch, steps, d = x_ref.shape
    sub_rows = batch * FFN_SUB_STEPS
    carry = (CONV_WIDTH - 1) * batch

    @pl.when(pl.program_id(0) == 0)
    def _():
        up_ref[sub_rows:sub_rows + carry, :] = jnp.zeros((carry, up_ref.shape[1]), _F32)

    for sub in range(steps // FFN_SUB_STEPS):
        t0 = sub * FFN_SUB_STEPS
        x = jnp.swapaxes(x_ref[:, t0:t0 + FFN_SUB_STEPS, :], 0, 1).reshape(sub_rows, d)
        h = (x * _rms_scale(x) * g_ref[...]).astype(_BF16)
        up_ref[0:carry, :] = up_ref[sub_rows:sub_rows + carry, :]

        def conv(lo, hi):
            up_ref[carry:carry + sub_rows, lo:hi] = jnp.dot(
                h, wup_ref[:, lo:hi], preferred_element_type=_F32)
            y = cb_ref[:, lo:hi]
            for tap in range(CONV_WIDTH):
                start = tap * batch
                y = y + up_ref[start:start + sub_rows, lo:hi] * cw_ref[tap:tap + 1, lo:hi]
            return y

        act = act_ref.at[sub % act_ref.shape[0]]
        for blk in range(D_FF // FF_BLOCK):
            lo, hi = blk * FF_BLOCK, (blk + 1) * FF_BLOCK
            a = conv(lo, hi)
            b = conv(D_FF + lo, D_FF + hi)
            act[:, lo:hi] = (a * jax.nn.sigmoid(a) * b).astype(_BF16)

        y = x + jnp.dot(act[...], wdown_ref[...], preferred_element_type=_F32)
        y = y * _rms_scale(y) * gf_ref[...]
        o_ref[:, t0:t0 + FFN_SUB_STEPS, :] = jnp.swapaxes(
            y.reshape(FFN_SUB_STEPS, batch, d), 0, 1)


def _const_spec(shape, grid_rank):
    zeros = (0,) * len(shape)
    index_map = (lambda j: zeros) if grid_rank == 1 else (lambda b, j: zeros)
    return pl.BlockSpec(shape, index_map, pipeline_mode=pl.Buffered(1))


def _as_f32_constants(tables):
    return tuple(jnp.asarray(t.astype(np.float32)) for t in tables)


def _decay_tables():
    log_gamma = np.log(1.0 - np.power(2.0, -5.0 - np.arange(RET_HEADS, dtype=np.float64)))
    pos = np.arange(CHUNK, dtype=np.float64)
    diff = pos[:, None] - pos[None, :]
    mask = np.where(diff >= 0.0,
                    np.exp(log_gamma[:, None, None] * np.maximum(diff, 0.0)[None]), 0.0)
    k_decay = np.exp(log_gamma[:, None] * (CHUNK - 1.0 - pos)[None])
    q_decay = np.exp(log_gamma[:, None] * (pos + 1.0)[None])
    chunk_decay = np.exp(log_gamma * CHUNK)
    widen = lambda t: np.repeat(t.T, HEAD_DIM, axis=1)
    return _as_f32_constants(
        (mask, widen(k_decay), widen(q_decay), np.repeat(chunk_decay, HEAD_DIM)[None, :]))


def _rope_tables(seq):
    half = HEAD_DIM // 2
    inv_freq = np.power(ROPE_BASE, -np.arange(half, dtype=np.float64) / half)
    ang = np.arange(seq, dtype=np.float64)[:, None] * inv_freq[None, :]
    cos, sin = np.cos(ang), np.sin(ang)
    cos_full = np.concatenate([cos, cos], axis=-1)
    sin_signed = np.concatenate([-sin, sin], axis=-1)
    scale = HEAD_DIM ** -0.5
    return _as_f32_constants((cos_full, sin_signed, cos_full * scale, sin_signed * scale))


def kernel(x, mix_norm_g, w_in, ret_norm_g, sgu_ln_g, sgu_ln_b, sgu_w_s, sgu_b_s, w_out,
           ffn_norm_g, w_up, conv_w, conv_b, w_down, final_norm_g):
    batch, seq, d_model = x.shape
    assert w_in.shape[0] == 1
    assert d_model == D_MODEL and seq % MIX_ROWS == 0 and MIX_ROWS % CHUNK == 0
    assert batch % MIX_BATCH == 0 and seq % FFN_STEPS == 0 and FFN_STEPS % FFN_SUB_STEPS == 0
    assert D_FF % FF_BLOCK == 0
    l = 0

    cos, sin, cosk, sink = _rope_tables(seq)
    mask, kdec, qdec, cdec = _decay_tables()
    out_shape = jax.ShapeDtypeStruct(x.shape, _F32)

    mix_grid = (batch // MIX_BATCH, seq // MIX_ROWS)
    mix_steps = mix_grid[0] * mix_grid[1]
    assert D_MODEL % (16 * mix_steps) == 0 and D_FF % (16 * mix_steps) == 0
    mix_x_spec = pl.BlockSpec((MIX_BATCH, MIX_ROWS, D_MODEL), lambda b, j: (b, j, 0))
    rope_spec = pl.BlockSpec((MIX_ROWS, HEAD_DIM), lambda b, j: (j, 0))
    wup_spec = pl.BlockSpec((D_MODEL // mix_steps, 2 * D_FF),
                            lambda b, j: (b * mix_grid[1] + j, 0))
    wdown_spec = pl.BlockSpec((D_FF // mix_steps, D_MODEL),
                              lambda b, j: (b * mix_grid[1] + j, 0))
    c2 = lambda shape: _const_spec(shape, 2)
    hbm_spec = pl.BlockSpec(memory_space=pl.ANY)
    assert w_in.shape[1] % CAST_ROWS == 0 and w_out.shape[1] % CAST_ROWS == 0
    x, w_up_bf, w_down_bf = pl.pallas_call(
        _mix_kernel,
        grid=mix_grid,
        in_specs=[mix_x_spec, c2((1, D_MODEL)), hbm_spec,
                  rope_spec, rope_spec, rope_spec, rope_spec,
                  c2(mask.shape), c2(kdec.shape), c2(qdec.shape), c2(cdec.shape),
                  c2((1, RET_WIDTH)), c2(sgu_ln_g.shape[1:]), c2(sgu_ln_b.shape[1:]),
                  c2(sgu_w_s.shape[1:]), c2(sgu_b_s.shape[1:]), hbm_spec,
                  wup_spec, wdown_spec],
        out_specs=[mix_x_spec, wup_spec, wdown_spec],
        out_shape=[out_shape, jax.ShapeDtypeStruct(w_up.shape[1:], _BF16),
                   jax.ShapeDtypeStruct(w_down.shape[1:], _BF16)],
        scratch_shapes=[pltpu.VMEM((MIX_BATCH, RET_HEADS, HEAD_DIM, HEAD_DIM), _F32),
                        pltpu.VMEM((MIX_BATCH, MIX_ROWS, RET_WIDTH + SGU_WIDTH), _BF16),
                        pltpu.VMEM((SGU_GROUPS, CHUNK, CHUNK), _BF16),
                        pltpu.VMEM((CHUNK, SGU_WIDTH), _F32),
                        pltpu.VMEM((2, MIX_ROWS, w_in.shape[2]), _F32),
                        pltpu.VMEM(w_in.shape[1:], _BF16),
                        pltpu.VMEM(w_out.shape[1:], _BF16),
                        pltpu.VMEM((2, CAST_ROWS, w_in.shape[2]), _F32),
                        pltpu.VMEM((2, CAST_ROWS, w_out.shape[2]), _F32),
                        pltpu.SemaphoreType.DMA((2,)),
                        pltpu.SemaphoreType.DMA((2,))],
        compiler_params=pltpu.CompilerParams(dimension_semantics=("arbitrary", "arbitrary"),
                                             vmem_limit_bytes=VMEM_LIMIT_BYTES),
        name="token_mix",
    )(x, mix_norm_g[l][None, :], w_in[l], cos, sin, cosk, sink,
      mask, kdec, qdec, cdec, ret_norm_g[l][None, :],
      sgu_ln_g[l], sgu_ln_b[l],
      sgu_w_s[l], sgu_b_s[l], w_out[l], w_up[l], w_down[l])

    sub_rows = batch * FFN_SUB_STEPS
    ffn_x_spec = pl.BlockSpec((batch, FFN_STEPS, D_MODEL), lambda j: (0, j, 0))
    c1 = lambda shape: _const_spec(shape, 1)
    return pl.pallas_call(
        _ffn_kernel,
        grid=(seq // FFN_STEPS,),
        in_specs=[ffn_x_spec, c1((1, D_MODEL)), c1(w_up.shape[1:]),
                  c1(conv_w.shape[1:]), c1((1, 2 * D_FF)),
                  c1(w_down.shape[1:]), c1((1, D_MODEL))],
        out_specs=ffn_x_spec,
        out_shape=out_shape,
        scratch_shapes=[pltpu.VMEM((sub_rows + (CONV_WIDTH - 1) * batch, 2 * D_FF), _F32),
                        pltpu.VMEM((2, sub_rows, D_FF), _BF16)],
        compiler_params=pltpu.CompilerParams(dimension_semantics=("arbitrary",),
                                             vmem_limit_bytes=VMEM_LIMIT_BYTES),
        name="channel_mix",
    )(x, ffn_norm_g[l][None, :], w_up_bf, conv_w[l], conv_b[l][None, :],
      w_down_bf, final_norm_g[None, :])
```

```python
import jax
import jax.numpy as jnp
import numpy as np
from jax import lax
from jax.experimental import pallas as pl
from jax.experimental.pallas import tpu as pltpu

D_MODEL = 1024
RET_HEADS = 4
HEAD_DIM = 128
RET_WIDTH = RET_HEADS * HEAD_DIM
SGU_GROUPS = 4
GROUP_DIM = 128
SGU_WIDTH = SGU_GROUPS * GROUP_DIM
CHUNK = 128
ROPE_BASE = 10000.0
D_FF = 2816
CONV_WIDTH = 3
EPS = 1e-6

MIX_ROWS = 256
MIX_BATCH = 4
FFN_STEPS = 64
FFN_SUB_STEPS = 32
FF_BLOCK = 256
VMEM_LIMIT_BYTES = 56 * 1024 * 1024

CAST_ROWS = 128
CAST_SLOTS = 4
OUT_PROJ_DELAY = 2
BF16_ROWS = 16

_F32 = jnp.float32
_BF16 = jnp.bfloat16
_SQRT_HALF = float(np.sqrt(0.5))


def _rms_scale(x):
    return lax.rsqrt(jnp.mean(x * x, axis=-1, keepdims=True) + EPS)


def _gelu(x):
    return 0.5 * x * (1.0 + lax.erf(x * _SQRT_HALF))


def _rotary(t, cos_full, sin_signed):
    return t * cos_full + pltpu.roll(t, HEAD_DIM // 2, axis=1) * sin_signed


def _cast_rows_to_bf16(src_hbm, dst_ref, stage_ref, sems):
    n_slots, rows, _ = stage_ref.shape
    n_chunks = src_hbm.shape[0] // rows

    def chunk_copy(c):
        return pltpu.make_async_copy(src_hbm.at[pl.ds(c * rows, rows), :],
                                     stage_ref.at[c % n_slots], sems.at[c % n_slots])

    for c in range(min(n_slots - 1, n_chunks)):
        chunk_copy(c).start()
    for c in range(n_chunks):
        if c + n_slots - 1 < n_chunks:
            chunk_copy(c + n_slots - 1).start()
        chunk_copy(c).wait()
        dst_ref[c * rows:(c + 1) * rows, :] = stage_ref[c % n_slots].astype(_BF16)


def _mix_kernel(x_ref, g_ref, win_hbm, cos_ref, sin_ref, cosk_ref, sink_ref,
                mask_ref, kdec_ref, qdec_ref, cdec_ref, retg_ref, lng_ref, lnb_ref,
                wsf_ref, bsr_ref, wout_hbm, wup_ref, wdown_ref,
                o_ref, wup_bf_ref, wdown_bf_ref, state_ref, mix_ref, ws_ref, bs_ref,
                proj_ref, win_ref, wout_ref, win_stage, wout_stage, win_sems, wout_sems):
    n_sub, rows, _ = x_ref.shape

    def cast_up(lo):
        wup_bf_ref[lo:lo + BF16_ROWS, :] = wup_ref[lo:lo + BF16_ROWS, :].astype(_BF16)

    def cast_down():
        wdown_bf_ref[...] = wdown_ref[...].astype(_BF16)

    cast_units = [lambda lo=lo: cast_up(lo) for lo in range(0, wup_ref.shape[0], BF16_ROWS)]
    cast_units.append(cast_down)

    @pl.when(pl.program_id(1) == 0)
    def _():
        state_ref[...] = jnp.zeros_like(state_ref)

    @pl.when(jnp.logical_and(pl.program_id(0) == 0, pl.program_id(1) == 0))
    def _():
        _cast_rows_to_bf16(win_hbm, win_ref, win_stage, win_sems)
        _cast_rows_to_bf16(wout_hbm, wout_ref, wout_stage, wout_sems)
        row = lax.broadcasted_iota(jnp.int32, (CHUNK, CHUNK), 0)
        col = lax.broadcasted_iota(jnp.int32, (CHUNK, CHUNK), 1)
        causal = (row >= col).astype(_F32)
        for gr in range(SGU_GROUPS):
            ws_ref[gr] = (wsf_ref[gr] * causal).astype(_BF16)
            bias_row = jnp.broadcast_to(bsr_ref[gr:gr + 1, :], (CHUNK, CHUNK))
            bias_col = jnp.sum(jnp.where(row == col, bias_row, 0.0), axis=-1, keepdims=True)
            bs_ref[:, gr * GROUP_DIM:(gr + 1) * GROUP_DIM] = jnp.broadcast_to(
                bias_col, (CHUNK, GROUP_DIM))

    cos, sin = cos_ref[...], sin_ref[...]
    cosk, sink = cosk_ref[...], sink_ref[...]

    def project(i):
        pb = proj_ref.at[i % proj_ref.shape[0]]
        cache = {}

        def unit(lo):
            if not cache:
                x = x_ref[i]
                cache["h"] = (x * _rms_scale(x) * g_ref[...]).astype(_BF16)
            pb[:, lo:lo + RET_WIDTH] = jnp.dot(cache["h"], win_ref[:, lo:lo + RET_WIDTH],
                                               preferred_element_type=_F32)

        return [lambda lo=lo: unit(lo) for lo in range(0, win_ref.shape[1], RET_WIDTH)]

    def mix(i):
        pb = proj_ref.at[i % proj_ref.shape[0]]
        q = pb.at[:, 0:RET_WIDTH]
        k = pb.at[:, RET_WIDTH:2 * RET_WIDTH]
        v = pb.at[:, 2 * RET_WIDTH:3 * RET_WIDTH]
        g = pb.at[:, 3 * RET_WIDTH:4 * RET_WIDTH]
        u = pb.at[:, 4 * RET_WIDTH:4 * RET_WIDTH + SGU_WIDTH]
        sv = pb.at[:, 4 * RET_WIDTH + SGU_WIDTH:4 * RET_WIDTH + 2 * SGU_WIDTH]

        def head(c, hd):
            r = slice(c * CHUNK, (c + 1) * CHUNK)
            l = slice(hd * HEAD_DIM, (hd + 1) * HEAD_DIM)
            qh = _rotary(q[r, l], cos[r], sin[r])
            kh = _rotary(k[r, l], cosk[r], sink[r])
            vh = v[r, l].astype(_BF16)
            scores = lax.dot_general(qh.astype(_BF16), kh.astype(_BF16),
                                     (((1,), (1,)), ((), ())),
                                     preferred_element_type=_F32) * mask_ref[hd]
            intra = jnp.dot(scores.astype(_BF16), vh, preferred_element_type=_F32)
            state = state_ref[i, hd]
            cross = jnp.dot((qh * qdec_ref[:, l]).astype(_BF16), state.astype(_BF16),
                            preferred_element_type=_F32)
            kv = lax.dot_general((kh * kdec_ref[:, l]).astype(_BF16), vh,
                                 (((0,), (0,)), ((), ())), preferred_element_type=_F32)
            state_ref[i, hd] = state * cdec_ref[:, l] + kv
            ret = intra + cross
            ret = ret * _rms_scale(ret) * retg_ref[:, l]
            gh = g[r, l]
            mix_ref[i, r, l] = (gh * jax.nn.sigmoid(gh) * ret).astype(_BF16)

        def group(c, gr):
            r = slice(c * CHUNK, (c + 1) * CHUNK)
            l = slice(gr * GROUP_DIM, (gr + 1) * GROUP_DIM)
            gv = _gelu(sv[r, l])
            xc = gv - jnp.mean(gv, axis=-1, keepdims=True)
            vn = xc * _rms_scale(xc) * lng_ref[gr:gr + 1, :] + lnb_ref[gr:gr + 1, :]
            mixed = jnp.dot(ws_ref[gr], vn.astype(_BF16), preferred_element_type=_F32)
            out = _gelu(u[r, l]) * (mixed + bs_ref[:, l])
            mix_ref[i, r, RET_WIDTH + gr * GROUP_DIM:RET_WIDTH + (gr + 1) * GROUP_DIM] = (
                out.astype(_BF16))

        def out_proj():
            o_ref[i] = x_ref[i] + jnp.dot(mix_ref[i], wout_ref[...],
                                          preferred_element_type=_F32)

        units = []
        for c in range(rows // CHUNK):
            units += [lambda c=c, hd=hd: head(c, hd) for hd in range(RET_HEADS)]
            units += [lambda c=c, gr=gr: group(c, gr) for gr in range(SGU_GROUPS)]
        return units + [out_proj]

    for unit in project(0):
        unit()
    pending_out = None
    for i in range(n_sub):
        *mix_units, out_proj = mix(i)
        if pending_out is not None:
            mix_units.insert(OUT_PROJ_DELAY, pending_out)
        pending_out = out_proj
        if i + 1 == n_sub:
            mix_units.append(pending_out)
        proj_units = project(i + 1) if i + 1 < n_sub else cast_units
        stride = -(-len(mix_units) // (len(proj_units) + 1))
        for n, unit in enumerate(mix_units):
            unit()
            if proj_units and (n + 1) % stride == 0:
                proj_units.pop(0)()
        for unit in proj_units:
            unit()


def _ffn_kernel(x_ref, g_ref, wup_ref, cw_ref, cb_ref, wdown_ref, gf_ref, o_ref,
                up_ref, act_ref):
    batch, steps, d = x_ref.shape
    sub_rows = batch * FFN_SUB_STEPS
    carry = (CONV_WIDTH - 1) * batch

    @pl.when(pl.program_id(0) == 0)
    def _():
        up_ref[sub_rows:sub_rows + carry, :] = jnp.zeros((carry, up_ref.shape[1]), _F32)

    for sub in range(steps // FFN_SUB_STEPS):
        t0 = sub * FFN_SUB_STEPS
        x = jnp.swapaxes(x_ref[:, t0:t0 + FFN_SUB_STEPS, :], 0, 1).reshape(sub_rows, d)
        h = (x * _rms_scale(x) * g_ref[...]).astype(_BF16)
        up_ref[0:carry, :] = up_ref[sub_rows:sub_rows + carry, :]

        def conv(lo, hi):
            up_ref[carry:carry + sub_rows, lo:hi] = jnp.dot(
                h, wup_ref[:, lo:hi], preferred_element_type=_F32)
            y = cb_ref[:, lo:hi]
            for tap in range(CONV_WIDTH):
                start = tap * batch
                y = y + up_ref[start:start + sub_rows, lo:hi] * cw_ref[tap:tap + 1, lo:hi]
            return y

        act = act_ref.at[sub % act_ref.shape[0]]
        for blk in range(D_FF // FF_BLOCK):
            lo, hi = blk * FF_BLOCK, (blk + 1) * FF_BLOCK
            a = conv(lo, hi)
            b = conv(D_FF + lo, D_FF + hi)
            act[:, lo:hi] = (a * jax.nn.sigmoid(a) * b).astype(_BF16)

        y = x + jnp.dot(act[...], wdown_ref[...], preferred_element_type=_F32)
        y = y * _rms_scale(y) * gf_ref[...]
        o_ref[:, t0:t0 + FFN_SUB_STEPS, :] = jnp.swapaxes(
            y.reshape(FFN_SUB_STEPS, batch, d), 0, 1)


def _const_spec(shape, grid_rank):
    zeros = (0,) * len(shape)
    index_map = (lambda j: zeros) if grid_rank == 1 else (lambda b, j: zeros)
    return pl.BlockSpec(shape, index_map, pipeline_mode=pl.Buffered(1))


def _as_f32_constants(tables):
    return tuple(jnp.asarray(t.astype(np.float32)) for t in tables)


def _decay_tables():
    log_gamma = np.log(1.0 - np.power(2.0, -5.0 - np.arange(RET_HEADS, dtype=np.float64)))
    pos = np.arange(CHUNK, dtype=np.float64)
    diff = pos[:, None] - pos[None, :]
    mask = np.where(diff >= 0.0,
                    np.exp(log_gamma[:, None, None] * np.maximum(diff, 0.0)[None]), 0.0)
    k_decay = np.exp(log_gamma[:, None] * (CHUNK - 1.0 - pos)[None])
    q_decay = np.exp(log_gamma[:, None] * (pos + 1.0)[None])
    chunk_decay = np.exp(log_gamma * CHUNK)
    widen = lambda t: np.repeat(t.T, HEAD_DIM, axis=1)
    return _as_f32_constants(
        (mask, widen(k_decay), widen(q_decay), np.repeat(chunk_decay, HEAD_DIM)[None, :]))


def _rope_tables(seq):
    half = HEAD_DIM // 2
    inv_freq = np.power(ROPE_BASE, -np.arange(half, dtype=np.float64) / half)
    ang = np.arange(seq, dtype=np.float64)[:, None] * inv_freq[None, :]
    cos, sin = np.cos(ang), np.sin(ang)
    cos_full = np.concatenate([cos, cos], axis=-1)
    sin_signed = np.concatenate([-sin, sin], axis=-1)
    scale = HEAD_DIM ** -0.5
    return _as_f32_constants((cos_full, sin_signed, cos_full * scale, sin_signed * scale))


def kernel(x, mix_norm_g, w_in, ret_norm_g, sgu_ln_g, sgu_ln_b, sgu_w_s, sgu_b_s, w_out,
           ffn_norm_g, w_up, conv_w, conv_b, w_down, final_norm_g):
    batch, seq, d_model = x.shape
    assert w_in.shape[0] == 1
    assert d_model == D_MODEL and seq % MIX_ROWS == 0 and MIX_ROWS % CHUNK == 0
    assert batch % MIX_BATCH == 0 and seq % FFN_STEPS == 0 and FFN_STEPS % FFN_SUB_STEPS == 0
    assert D_FF % FF_BLOCK == 0
    l = 0

    cos, sin, cosk, sink = _rope_tables(seq)
    mask, kdec, qdec, cdec = _decay_tables()
    out_shape = jax.ShapeDtypeStruct(x.shape, _F32)

    mix_grid = (batch // MIX_BATCH, seq // MIX_ROWS)
    mix_steps = mix_grid[0] * mix_grid[1]
    assert D_MODEL % (16 * mix_steps) == 0 and D_FF % (16 * mix_steps) == 0
    mix_x_spec = pl.BlockSpec((MIX_BATCH, MIX_ROWS, D_MODEL), lambda b, j: (b, j, 0))
    rope_spec = pl.BlockSpec((MIX_ROWS, HEAD_DIM), lambda b, j: (j, 0))
    wup_spec = pl.BlockSpec((D_MODEL // mix_steps, 2 * D_FF),
                            lambda b, j: (b * mix_grid[1] + j, 0))
    wdown_spec = pl.BlockSpec((D_FF // mix_steps, D_MODEL),
                              lambda b, j: (b * mix_grid[1] + j, 0))
    c2 = lambda shape: _const_spec(shape, 2)
    hbm_spec = pl.BlockSpec(memory_space=pl.ANY)
    assert w_in.shape[1] % CAST_ROWS == 0 and w_out.shape[1] % CAST_ROWS == 0
    x, w_up_bf, w_down_bf = pl.pallas_call(
        _mix_kernel,
        grid=mix_grid,
        in_specs=[mix_x_spec, c2((1, D_MODEL)), hbm_spec,
                  rope_spec, rope_spec, rope_spec, rope_spec,
                  c2(mask.shape), c2(kdec.shape), c2(qdec.shape), c2(cdec.shape),
                  c2((1, RET_WIDTH)), c2(sgu_ln_g.shape[1:]), c2(sgu_ln_b.shape[1:]),
                  c2(sgu_w_s.shape[1:]), c2(sgu_b_s.shape[1:]), hbm_spec,
                  wup_spec, wdown_spec],
        out_specs=[mix_x_spec, wup_spec, wdown_spec],
        out_shape=[out_shape, jax.ShapeDtypeStruct(w_up.shape[1:], _BF16),
                   jax.ShapeDtypeStruct(w_down.shape[1:], _BF16)],
        scratch_shapes=[pltpu.VMEM((MIX_BATCH, RET_HEADS, HEAD_DIM, HEAD_DIM), _F32),
                        pltpu.VMEM((MIX_BATCH, MIX_ROWS, RET_WIDTH + SGU_WIDTH), _BF16),
                        pltpu.VMEM((SGU_GROUPS, CHUNK, CHUNK), _BF16),
                        pltpu.VMEM((CHUNK, SGU_WIDTH), _F32),
                        pltpu.VMEM((2, MIX_ROWS, w_in.shape[2]), _F32),
                        pltpu.VMEM(w_in.shape[1:], _BF16),
                        pltpu.VMEM(w_out.shape[1:], _BF16),
                        pltpu.VMEM((CAST_SLOTS, CAST_ROWS, w_in.shape[2]), _F32),
                        pltpu.VMEM((CAST_SLOTS, CAST_ROWS, w_out.shape[2]), _F32),
                        pltpu.SemaphoreType.DMA((CAST_SLOTS,)),
                        pltpu.SemaphoreType.DMA((CAST_SLOTS,))],
        compiler_params=pltpu.CompilerParams(dimension_semantics=("arbitrary", "arbitrary"),
                                             vmem_limit_bytes=VMEM_LIMIT_BYTES),
        name="token_mix",
    )(x, mix_norm_g[l][None, :], w_in[l], cos, sin, cosk, sink,
      mask, kdec, qdec, cdec, ret_norm_g[l][None, :],
      sgu_ln_g[l], sgu_ln_b[l],
      sgu_w_s[l], sgu_b_s[l], w_out[l], w_up[l], w_down[l])

    sub_rows = batch * FFN_SUB_STEPS
    ffn_x_spec = pl.BlockSpec((batch, FFN_STEPS, D_MODEL), lambda j: (0, j, 0))
    c1 = lambda shape: _const_spec(shape, 1)
    return pl.pallas_call(
        _ffn_kernel,
        grid=(seq // FFN_STEPS,),
        in_specs=[ffn_x_spec, c1((1, D_MODEL)), c1(w_up.shape[1:]),
                  c1(conv_w.shape[1:]), c1((1, 2 * D_FF)),
                  c1(w_down.shape[1:]), c1((1, D_MODEL))],
        out_specs=ffn_x_spec,
        out_shape=out_shape,
        scratch_shapes=[pltpu.VMEM((sub_rows + (CONV_WIDTH - 1) * batch, 2 * D_FF), _F32),
                        pltpu.VMEM((2, sub_rows, D_FF), _BF16)],
        compiler_params=pltpu.CompilerParams(dimension_semantics=("arbitrary",),
                                             vmem_limit_bytes=VMEM_LIMIT_BYTES),
        name="channel_mix",
    )(x, ffn_norm_g[l][None, :], w_up_bf, conv_w[l], conv_b[l][None, :],
      w_down_bf, final_norm_g[None, :])
```

```python
import jax
import jax.numpy as jnp
import numpy as np
from jax import lax
from jax.experimental import pallas as pl
from jax.experimental.pallas import tpu as pltpu

D_MODEL = 1024
RET_HEADS = 4
HEAD_DIM = 128
RET_WIDTH = RET_HEADS * HEAD_DIM
SGU_GROUPS = 4
GROUP_DIM = 128
SGU_WIDTH = SGU_GROUPS * GROUP_DIM
CHUNK = 128
ROPE_BASE = 10000.0
D_FF = 2816
CONV_WIDTH = 3
EPS = 1e-6

MIX_ROWS = 256
RET_CHUNK = MIX_ROWS
MIX_BATCH = 4
FFN_STEPS = 64
FFN_SUB_STEPS = 32
FF_BLOCK = 256
VMEM_LIMIT_BYTES = 56 * 1024 * 1024

CAST_ROWS = 128
CAST_SLOTS = 4
OUT_PROJ_DELAY = 2
BF16_ROWS = 16

_F32 = jnp.float32
_BF16 = jnp.bfloat16
_SQRT_HALF = float(np.sqrt(0.5))


def _rms_scale(x):
    return lax.rsqrt(jnp.mean(x * x, axis=-1, keepdims=True) + EPS)


def _gelu(x):
    return 0.5 * x * (1.0 + lax.erf(x * _SQRT_HALF))


def _rotary(t, cos_full, sin_signed):
    return t * cos_full + pltpu.roll(t, HEAD_DIM // 2, axis=1) * sin_signed


def _cast_rows_to_bf16(src_hbm, dst_ref, stage_ref, sems):
    n_slots, rows, _ = stage_ref.shape
    n_chunks = src_hbm.shape[0] // rows

    def chunk_copy(c):
        return pltpu.make_async_copy(src_hbm.at[pl.ds(c * rows, rows), :],
                                     stage_ref.at[c % n_slots], sems.at[c % n_slots])

    for c in range(min(n_slots - 1, n_chunks)):
        chunk_copy(c).start()
    for c in range(n_chunks):
        if c + n_slots - 1 < n_chunks:
            chunk_copy(c + n_slots - 1).start()
        chunk_copy(c).wait()
        dst_ref[c * rows:(c + 1) * rows, :] = stage_ref[c % n_slots].astype(_BF16)


def _mix_kernel(x_ref, g_ref, win_hbm, cos_ref, sin_ref, cosk_ref, sink_ref,
                mask_ref, kdec_ref, qdec_ref, cdec_ref, retg_ref, lng_ref, lnb_ref,
                wsf_ref, bsr_ref, wout_hbm, wup_ref, wdown_ref,
                o_ref, wup_bf_ref, wdown_bf_ref, state_ref, mix_ref, ws_ref, bs_ref,
                proj_ref, win_ref, wout_ref, win_stage, wout_stage, win_sems, wout_sems):
    n_sub, rows, _ = x_ref.shape

    def cast_up(lo):
        wup_bf_ref[lo:lo + BF16_ROWS, :] = wup_ref[lo:lo + BF16_ROWS, :].astype(_BF16)

    def cast_down():
        wdown_bf_ref[...] = wdown_ref[...].astype(_BF16)

    cast_units = [lambda lo=lo: cast_up(lo) for lo in range(0, wup_ref.shape[0], BF16_ROWS)]
    cast_units.append(cast_down)

    @pl.when(pl.program_id(1) == 0)
    def _():
        state_ref[...] = jnp.zeros_like(state_ref)

    @pl.when(jnp.logical_and(pl.program_id(0) == 0, pl.program_id(1) == 0))
    def _():
        _cast_rows_to_bf16(win_hbm, win_ref, win_stage, win_sems)
        _cast_rows_to_bf16(wout_hbm, wout_ref, wout_stage, wout_sems)
        row = lax.broadcasted_iota(jnp.int32, (CHUNK, CHUNK), 0)
        col = lax.broadcasted_iota(jnp.int32, (CHUNK, CHUNK), 1)
        causal = (row >= col).astype(_F32)
        for gr in range(SGU_GROUPS):
            ws_ref[gr] = (wsf_ref[gr] * causal).astype(_BF16)
            bias_row = jnp.broadcast_to(bsr_ref[gr:gr + 1, :], (CHUNK, CHUNK))
            bias_col = jnp.sum(jnp.where(row == col, bias_row, 0.0), axis=-1, keepdims=True)
            bs_ref[:, gr * GROUP_DIM:(gr + 1) * GROUP_DIM] = jnp.broadcast_to(
                bias_col, (CHUNK, GROUP_DIM))

    cos, sin = cos_ref[...], sin_ref[...]
    cosk, sink = cosk_ref[...], sink_ref[...]

    def project(i):
        pb = proj_ref.at[i % proj_ref.shape[0]]
        cache = {}

        def unit(lo):
            if not cache:
                x = x_ref[i]
                cache["h"] = (x * _rms_scale(x) * g_ref[...]).astype(_BF16)
            pb[:, lo:lo + RET_WIDTH] = jnp.dot(cache["h"], win_ref[:, lo:lo + RET_WIDTH],
                                               preferred_element_type=_F32)

        return [lambda lo=lo: unit(lo) for lo in range(0, win_ref.shape[1], RET_WIDTH)]

    def mix(i):
        pb = proj_ref.at[i % proj_ref.shape[0]]
        q = pb.at[:, 0:RET_WIDTH]
        k = pb.at[:, RET_WIDTH:2 * RET_WIDTH]
        v = pb.at[:, 2 * RET_WIDTH:3 * RET_WIDTH]
        g = pb.at[:, 3 * RET_WIDTH:4 * RET_WIDTH]
        u = pb.at[:, 4 * RET_WIDTH:4 * RET_WIDTH + SGU_WIDTH]
        sv = pb.at[:, 4 * RET_WIDTH + SGU_WIDTH:4 * RET_WIDTH + 2 * SGU_WIDTH]

        def head(c, hd):
            r = slice(c * RET_CHUNK, (c + 1) * RET_CHUNK)
            l = slice(hd * HEAD_DIM, (hd + 1) * HEAD_DIM)
            qh = _rotary(q[r, l], cos[r], sin[r])
            kh = _rotary(k[r, l], cosk[r], sink[r])
            vh = v[r, l].astype(_BF16)
            scores = lax.dot_general(qh.astype(_BF16), kh.astype(_BF16),
                                     (((1,), (1,)), ((), ())),
                                     preferred_element_type=_F32) * mask_ref[hd]
            intra = jnp.dot(scores.astype(_BF16), vh, preferred_element_type=_F32)
            state = state_ref[i, hd]
            cross = jnp.dot((qh * qdec_ref[:, l]).astype(_BF16), state.astype(_BF16),
                            preferred_element_type=_F32)
            kv = lax.dot_general((kh * kdec_ref[:, l]).astype(_BF16), vh,
                                 (((0,), (0,)), ((), ())), preferred_element_type=_F32)
            state_ref[i, hd] = state * cdec_ref[:, l] + kv
            ret = intra + cross
            ret = ret * _rms_scale(ret) * retg_ref[:, l]
            gh = g[r, l]
            mix_ref[i, r, l] = (gh * jax.nn.sigmoid(gh) * ret).astype(_BF16)

        def group(c, gr):
            r = slice(c * CHUNK, (c + 1) * CHUNK)
            l = slice(gr * GROUP_DIM, (gr + 1) * GROUP_DIM)
            gv = _gelu(sv[r, l])
            xc = gv - jnp.mean(gv, axis=-1, keepdims=True)
            vn = xc * _rms_scale(xc) * lng_ref[gr:gr + 1, :] + lnb_ref[gr:gr + 1, :]
            mixed = jnp.dot(ws_ref[gr], vn.astype(_BF16), preferred_element_type=_F32)
            out = _gelu(u[r, l]) * (mixed + bs_ref[:, l])
            mix_ref[i, r, RET_WIDTH + gr * GROUP_DIM:RET_WIDTH + (gr + 1) * GROUP_DIM] = (
                out.astype(_BF16))

        def out_proj():
            o_ref[i] = x_ref[i] + jnp.dot(mix_ref[i], wout_ref[...],
                                          preferred_element_type=_F32)

        heads = [lambda c=c, hd=hd: head(c, hd)
                 for c in range(rows // RET_CHUNK) for hd in range(RET_HEADS)]
        groups = [lambda c=c, gr=gr: group(c, gr)
                  for c in range(rows // CHUNK) for gr in range(SGU_GROUPS)]
        units = []
        for n in range(max(len(heads), len(groups))):
            units += heads[n:n + 1] + groups[n:n + 1]
        return units + [out_proj]

    for unit in project(0):
        unit()
    pending_out = None
    for i in range(n_sub):
        *mix_units, out_proj = mix(i)
        if pending_out is not None:
            mix_units.insert(OUT_PROJ_DELAY, pending_out)
        pending_out = out_proj
        if i + 1 == n_sub:
            mix_units.append(pending_out)
        proj_units = project(i + 1) if i + 1 < n_sub else cast_units
        stride = -(-len(mix_units) // (len(proj_units) + 1))
        for n, unit in enumerate(mix_units):
            unit()
            if proj_units and (n + 1) % stride == 0:
                proj_units.pop(0)()
        for unit in proj_units:
            unit()


def _ffn_kernel(x_ref, g_ref, wup_ref, cw_ref, cb_ref, wdown_ref, gf_ref, o_ref,
                up_ref, act_ref):
    batch, steps, d = x_ref.shape
    sub_rows = batch * FFN_SUB_STEPS
    carry = (CONV_WIDTH - 1) * batch

    @pl.when(pl.program_id(0) == 0)
    def _():
        up_ref[sub_rows:sub_rows + carry, :] = jnp.zeros((carry, up_ref.shape[1]), _F32)

    for sub in range(steps // FFN_SUB_STEPS):
        t0 = sub * FFN_SUB_STEPS
        x = jnp.swapaxes(x_ref[:, t0:t0 + FFN_SUB_STEPS, :], 0, 1).reshape(sub_rows, d)
        h = (x * _rms_scale(x) * g_ref[...]).astype(_BF16)
        up_ref[0:carry, :] = up_ref[sub_rows:sub_rows + carry, :]

        def conv(lo, hi):
            up_ref[carry:carry + sub_rows, lo:hi] = jnp.dot(
                h, wup_ref[:, lo:hi], preferred_element_type=_F32)
            y = cb_ref[:, lo:hi]
            for tap in range(CONV_WIDTH):
                start = tap * batch
                y = y + up_ref[start:start + sub_rows, lo:hi] * cw_ref[tap:tap + 1, lo:hi]
            return y

        act = act_ref.at[sub % act_ref.shape[0]]
        for blk in range(D_FF // FF_BLOCK):
            lo, hi = blk * FF_BLOCK, (blk + 1) * FF_BLOCK
            a = conv(lo, hi)
            b = conv(D_FF + lo, D_FF + hi)
            act[:, lo:hi] = (a * jax.nn.sigmoid(a) * b).astype(_BF16)

        y = x + jnp.dot(act[...], wdown_ref[...], preferred_element_type=_F32)
        y = y * _rms_scale(y) * gf_ref[...]
        o_ref[:, t0:t0 + FFN_SUB_STEPS, :] = jnp.swapaxes(
            y.reshape(FFN_SUB_STEPS, batch, d), 0, 1)


def _const_spec(shape, grid_rank):
    zeros = (0,) * len(shape)
    index_map = (lambda j: zeros) if grid_rank == 1 else (lambda b, j: zeros)
    return pl.BlockSpec(shape, index_map, pipeline_mode=pl.Buffered(1))


def _as_f32_constants(tables):
    return tuple(jnp.asarray(t.astype(np.float32)) for t in tables)


def _decay_tables():
    log_gamma = np.log(1.0 - np.power(2.0, -5.0 - np.arange(RET_HEADS, dtype=np.float64)))
    pos = np.arange(RET_CHUNK, dtype=np.float64)
    diff = pos[:, None] - pos[None, :]
    mask = np.where(diff >= 0.0,
                    np.exp(log_gamma[:, None, None] * np.maximum(diff, 0.0)[None]), 0.0)
    k_decay = np.exp(log_gamma[:, None] * (RET_CHUNK - 1.0 - pos)[None])
    q_decay = np.exp(log_gamma[:, None] * (pos + 1.0)[None])
    chunk_decay = np.exp(log_gamma * RET_CHUNK)
    widen = lambda t: np.repeat(t.T, HEAD_DIM, axis=1)
    return _as_f32_constants(
        (mask, widen(k_decay), widen(q_decay), np.repeat(chunk_decay, HEAD_DIM)[None, :]))


def _rope_tables(seq):
    half = HEAD_DIM // 2
    inv_freq = np.power(ROPE_BASE, -np.arange(half, dtype=np.float64) / half)
    ang = np.arange(seq, dtype=np.float64)[:, None] * inv_freq[None, :]
    cos, sin = np.cos(ang), np.sin(ang)
    cos_full = np.concatenate([cos, cos], axis=-1)
    sin_signed = np.concatenate([-sin, sin], axis=-1)
    scale = HEAD_DIM ** -0.5
    return _as_f32_constants((cos_full, sin_signed, cos_full * scale, sin_signed * scale))


def kernel(x, mix_norm_g, w_in, ret_norm_g, sgu_ln_g, sgu_ln_b, sgu_w_s, sgu_b_s, w_out,
           ffn_norm_g, w_up, conv_w, conv_b, w_down, final_norm_g):
    batch, seq, d_model = x.shape
    assert w_in.shape[0] == 1
    assert d_model == D_MODEL and seq % MIX_ROWS == 0 and MIX_ROWS % CHUNK == 0
    assert batch % MIX_BATCH == 0 and seq % FFN_STEPS == 0 and FFN_STEPS % FFN_SUB_STEPS == 0
    assert D_FF % FF_BLOCK == 0
    l = 0

    cos, sin, cosk, sink = _rope_tables(seq)
    mask, kdec, qdec, cdec = _decay_tables()
    out_shape = jax.ShapeDtypeStruct(x.shape, _F32)

    mix_grid = (batch // MIX_BATCH, seq // MIX_ROWS)
    mix_steps = mix_grid[0] * mix_grid[1]
    assert D_MODEL % (16 * mix_steps) == 0 and D_FF % (16 * mix_steps) == 0
    mix_x_spec = pl.BlockSpec((MIX_BATCH, MIX_ROWS, D_MODEL), lambda b, j: (b, j, 0))
    rope_spec = pl.BlockSpec((MIX_ROWS, HEAD_DIM), lambda b, j: (j, 0))
    wup_spec = pl.BlockSpec((D_MODEL // mix_steps, 2 * D_FF),
                            lambda b, j: (b * mix_grid[1] + j, 0))
    wdown_spec = pl.BlockSpec((D_FF // mix_steps, D_MODEL),
                              lambda b, j: (b * mix_grid[1] + j, 0))
    c2 = lambda shape: _const_spec(shape, 2)
    hbm_spec = pl.BlockSpec(memory_space=pl.ANY)
    assert w_in.shape[1] % CAST_ROWS == 0 and w_out.shape[1] % CAST_ROWS == 0
    x, w_up_bf, w_down_bf = pl.pallas_call(
        _mix_kernel,
        grid=mix_grid,
        in_specs=[mix_x_spec, c2((1, D_MODEL)), hbm_spec,
                  rope_spec, rope_spec, rope_spec, rope_spec,
                  c2(mask.shape), c2(kdec.shape), c2(qdec.shape), c2(cdec.shape),
                  c2((1, RET_WIDTH)), c2(sgu_ln_g.shape[1:]), c2(sgu_ln_b.shape[1:]),
                  c2(sgu_w_s.shape[1:]), c2(sgu_b_s.shape[1:]), hbm_spec,
                  wup_spec, wdown_spec],
        out_specs=[mix_x_spec, wup_spec, wdown_spec],
        out_shape=[out_shape, jax.ShapeDtypeStruct(w_up.shape[1:], _BF16),
                   jax.ShapeDtypeStruct(w_down.shape[1:], _BF16)],
        scratch_shapes=[pltpu.VMEM((MIX_BATCH, RET_HEADS, HEAD_DIM, HEAD_DIM), _F32),
                        pltpu.VMEM((MIX_BATCH, MIX_ROWS, RET_WIDTH + SGU_WIDTH), _BF16),
                        pltpu.VMEM((SGU_GROUPS, CHUNK, CHUNK), _BF16),
                        pltpu.VMEM((CHUNK, SGU_WIDTH), _F32),
                        pltpu.VMEM((2, MIX_ROWS, w_in.shape[2]), _F32),
                        pltpu.VMEM(w_in.shape[1:], _BF16),
                        pltpu.VMEM(w_out.shape[1:], _BF16),
                        pltpu.VMEM((CAST_SLOTS, CAST_ROWS, w_in.shape[2]), _F32),
                        pltpu.VMEM((CAST_SLOTS, CAST_ROWS, w_out.shape[2]), _F32),
                        pltpu.SemaphoreType.DMA((CAST_SLOTS,)),
                        pltpu.SemaphoreType.DMA((CAST_SLOTS,))],
        compiler_params=pltpu.CompilerParams(dimension_semantics=("arbitrary", "arbitrary"),
                                             vmem_limit_bytes=VMEM_LIMIT_BYTES),
        name="token_mix",
    )(x, mix_norm_g[l][None, :], w_in[l], cos, sin, cosk, sink,
      mask, kdec, qdec, cdec, ret_norm_g[l][None, :],
      sgu_ln_g[l], sgu_ln_b[l],
      sgu_w_s[l], sgu_b_s[l], w_out[l], w_up[l], w_down[l])

    sub_rows = batch * FFN_SUB_STEPS
    ffn_x_spec = pl.BlockSpec((batch, FFN_STEPS, D_MODEL), lambda j: (0, j, 0))
    c1 = lambda shape: _const_spec(shape, 1)
    return pl.pallas_call(
        _ffn_kernel,
        grid=(seq // FFN_STEPS,),
        in_specs=[ffn_x_spec, c1((1, D_MODEL)), c1(w_up.shape[1:]),
                  c1(conv_w.shape[1:]), c1((1, 2 * D_FF)),
                  c1(w_down.shape[1:]), c1((1, D_MODEL))],
        out_specs=ffn_x_spec,
        out_shape=out_shape,
        scratch_shapes=[pltpu.VMEM((sub_rows + (CONV_WIDTH - 1) * batch, 2 * D_FF), _F32),
                        pltpu.VMEM((2, sub_rows, D_FF), _BF16)],
        compiler_params=pltpu.CompilerParams(dimension_semantics=("arbitrary",),
                                             vmem_limit_bytes=VMEM_LIMIT_BYTES),
        name="channel_mix",
    )(x, ffn_norm_g[l][None, :], w_up_bf, conv_w[l], conv_b[l][None, :],
      w_down_bf, final_norm_g[None, :])
```

```python
import jax
import jax.numpy as jnp
import numpy as np
from jax import lax
from jax.experimental import pallas as pl
from jax.experimental.pallas import tpu as pltpu

D_MODEL = 1024
RET_HEADS = 4
HEAD_DIM = 128
RET_WIDTH = RET_HEADS * HEAD_DIM
SGU_GROUPS = 4
GROUP_DIM = 128
SGU_WIDTH = SGU_GROUPS * GROUP_DIM
CHUNK = 128
ROPE_BASE = 10000.0
D_FF = 2816
CONV_WIDTH = 3
EPS = 1e-6

MIX_ROWS = 256
RET_CHUNK = 128
MIX_BATCH = 4
FFN_STEPS = 64
FFN_SUB_STEPS = 32
FF_BLOCK = 256
VMEM_LIMIT_BYTES = 56 * 1024 * 1024

CAST_ROWS = 128
CAST_SLOTS = 4
OUT_PROJ_DELAY = 2
BF16_ROWS = 16

_F32 = jnp.float32
_BF16 = jnp.bfloat16
_SQRT_HALF = float(np.sqrt(0.5))


def _rms_scale(x):
    return lax.rsqrt(jnp.mean(x * x, axis=-1, keepdims=True) + EPS)


def _gelu(x):
    return 0.5 * x * (1.0 + lax.erf(x * _SQRT_HALF))


def _rotary(t, cos_full, sin_signed):
    return t * cos_full + pltpu.roll(t, HEAD_DIM // 2, axis=1) * sin_signed


def _cast_rows_to_bf16(src_hbm, dst_ref, stage_ref, sems):
    n_slots, rows, _ = stage_ref.shape
    n_chunks = src_hbm.shape[0] // rows

    def chunk_copy(c):
        return pltpu.make_async_copy(src_hbm.at[pl.ds(c * rows, rows), :],
                                     stage_ref.at[c % n_slots], sems.at[c % n_slots])

    for c in range(min(n_slots - 1, n_chunks)):
        chunk_copy(c).start()
    for c in range(n_chunks):
        if c + n_slots - 1 < n_chunks:
            chunk_copy(c + n_slots - 1).start()
        chunk_copy(c).wait()
        dst_ref[c * rows:(c + 1) * rows, :] = stage_ref[c % n_slots].astype(_BF16)


def _mix_kernel(x_ref, g_ref, win_hbm, cos_ref, sin_ref, cosk_ref, sink_ref,
                mask_ref, kdec_ref, qdec_ref, cdec_ref, retg_ref, lng_ref, lnb_ref,
                wsf_ref, bsr_ref, wout_hbm, wup_ref, wdown_ref,
                o_ref, wup_bf_ref, wdown_bf_ref, state_ref, mix_ref, ws_ref, bs_ref,
                proj_ref, win_ref, wout_ref, win_stage, wout_stage, win_sems, wout_sems):
    n_sub, rows, _ = x_ref.shape

    def cast_up(lo):
        wup_bf_ref[lo:lo + BF16_ROWS, :] = wup_ref[lo:lo + BF16_ROWS, :].astype(_BF16)

    def cast_down():
        wdown_bf_ref[...] = wdown_ref[...].astype(_BF16)

    cast_units = [lambda lo=lo: cast_up(lo) for lo in range(0, wup_ref.shape[0], BF16_ROWS)]
    cast_units.append(cast_down)

    @pl.when(pl.program_id(1) == 0)
    def _():
        state_ref[...] = jnp.zeros_like(state_ref)

    @pl.when(jnp.logical_and(pl.program_id(0) == 0, pl.program_id(1) == 0))
    def _():
        _cast_rows_to_bf16(win_hbm, win_ref, win_stage, win_sems)
        _cast_rows_to_bf16(wout_hbm, wout_ref, wout_stage, wout_sems)
        row = lax.broadcasted_iota(jnp.int32, (CHUNK, CHUNK), 0)
        col = lax.broadcasted_iota(jnp.int32, (CHUNK, CHUNK), 1)
        causal = (row >= col).astype(_F32)
        for gr in range(SGU_GROUPS):
            ws_ref[gr] = (wsf_ref[gr] * causal).astype(_BF16)
            bias_row = jnp.broadcast_to(bsr_ref[gr:gr + 1, :], (CHUNK, CHUNK))
            bias_col = jnp.sum(jnp.where(row == col, bias_row, 0.0), axis=-1, keepdims=True)
            bs_ref[:, gr * GROUP_DIM:(gr + 1) * GROUP_DIM] = jnp.broadcast_to(
                bias_col, (CHUNK, GROUP_DIM))

    cos, sin = cos_ref[...], sin_ref[...]
    cosk, sink = cosk_ref[...], sink_ref[...]

    def project(i):
        pb = proj_ref.at[i % proj_ref.shape[0]]
        cache = {}

        def unit(lo):
            if not cache:
                x = x_ref[i]
                cache["h"] = (x * _rms_scale(x) * g_ref[...]).astype(_BF16)
            pb[:, lo:lo + RET_WIDTH] = jnp.dot(cache["h"], win_ref[:, lo:lo + RET_WIDTH],
                                               preferred_element_type=_F32)

        return [lambda lo=lo: unit(lo) for lo in range(0, win_ref.shape[1], RET_WIDTH)]

    def mix(i):
        pb = proj_ref.at[i % proj_ref.shape[0]]
        q = pb.at[:, 0:RET_WIDTH]
        k = pb.at[:, RET_WIDTH:2 * RET_WIDTH]
        v = pb.at[:, 2 * RET_WIDTH:3 * RET_WIDTH]
        g = pb.at[:, 3 * RET_WIDTH:4 * RET_WIDTH]
        u = pb.at[:, 4 * RET_WIDTH:4 * RET_WIDTH + SGU_WIDTH]
        sv = pb.at[:, 4 * RET_WIDTH + SGU_WIDTH:4 * RET_WIDTH + 2 * SGU_WIDTH]

        def head(c, hd):
            r = slice(c * RET_CHUNK, (c + 1) * RET_CHUNK)
            l = slice(hd * HEAD_DIM, (hd + 1) * HEAD_DIM)
            qh = _rotary(q[r, l], cos[r], sin[r])
            kh = _rotary(k[r, l], cosk[r], sink[r])
            vh = v[r, l].astype(_BF16)
            scores = lax.dot_general(qh.astype(_BF16), kh.astype(_BF16),
                                     (((1,), (1,)), ((), ())),
                                     preferred_element_type=_F32) * mask_ref[hd]
            intra = jnp.dot(scores.astype(_BF16), vh, preferred_element_type=_F32)
            state = state_ref[i, hd]
            cross = jnp.dot((qh * qdec_ref[:, l]).astype(_BF16), state.astype(_BF16),
                            preferred_element_type=_F32)
            kv = lax.dot_general((kh * kdec_ref[:, l]).astype(_BF16), vh,
                                 (((0,), (0,)), ((), ())), preferred_element_type=_F32)
            state_ref[i, hd] = state * cdec_ref[:, l] + kv
            ret = intra + cross
            ret = ret * _rms_scale(ret) * retg_ref[:, l]
            gh = g[r, l]
            mix_ref[i, r, l] = (gh * jax.nn.sigmoid(gh) * ret).astype(_BF16)

        def group(c, gr):
            r = slice(c * CHUNK, (c + 1) * CHUNK)
            l = slice(gr * GROUP_DIM, (gr + 1) * GROUP_DIM)
            gv = _gelu(sv[r, l])
            xc = gv - jnp.mean(gv, axis=-1, keepdims=True)
            vn = xc * _rms_scale(xc) * lng_ref[gr:gr + 1, :] + lnb_ref[gr:gr + 1, :]
            mixed = jnp.dot(ws_ref[gr], vn.astype(_BF16), preferred_element_type=_F32)
            out = _gelu(u[r, l]) * (mixed + bs_ref[:, l])
            mix_ref[i, r, RET_WIDTH + gr * GROUP_DIM:RET_WIDTH + (gr + 1) * GROUP_DIM] = (
                out.astype(_BF16))

        def out_proj():
            o_ref[i] = x_ref[i] + jnp.dot(mix_ref[i], wout_ref[...],
                                          preferred_element_type=_F32)

        units = []
        for c in range(rows // CHUNK):
            units += [lambda c=c, hd=hd: head(c, hd) for hd in range(RET_HEADS)]
            units += [lambda c=c, gr=gr: group(c, gr) for gr in range(SGU_GROUPS)]
        return units + [out_proj]

    for unit in project(0):
        unit()
    pending_out = None
    for i in range(n_sub):
        *mix_units, out_proj = mix(i)
        if pending_out is not None:
            mix_units.insert(OUT_PROJ_DELAY, pending_out)
        pending_out = out_proj
        if i + 1 == n_sub:
            mix_units.append(pending_out)
        proj_units = project(i + 1) if i + 1 < n_sub else cast_units
        stride = -(-len(mix_units) // (len(proj_units) + 1))
        for n, unit in enumerate(mix_units):
            unit()
            if proj_units and (n + 1) % stride == 0:
                proj_units.pop(0)()
        for unit in proj_units:
            unit()


def _ffn_kernel(x_hbm, g_ref, wup_ref, cw_ref, cb_ref, wdown_ref, gf_ref, o_hbm,
                up_ref, act_ref, xin_ref, yout_ref, in_sems, out_sems):
    _, steps, batch, d = xin_ref.shape
    n_sub = xin_ref.shape[0] // 2
    sub_rows = batch * steps
    carry = (CONV_WIDTH - 1) * batch
    step = pl.program_id(0)
    n_steps = pl.num_programs(0)

    def tile_copies(s, sub, incoming):
        slot = (s % 2) * n_sub + sub
        t = pl.ds((s * n_sub + sub) * steps, steps)
        if incoming:
            return [pltpu.make_async_copy(x_hbm.at[b, t, :], xin_ref.at[slot, :, b, :],
                                          in_sems.at[slot]) for b in range(batch)]
        return [pltpu.make_async_copy(yout_ref.at[slot, :, b, :], o_hbm.at[b, t, :],
                                      out_sems.at[slot]) for b in range(batch)]

    def for_tiles(s, incoming, action):
        for sub in range(n_sub):
            for copy in tile_copies(s, sub, incoming):
                action(copy)

    start = lambda copy: copy.start()
    wait = lambda copy: copy.wait()

    @pl.when(step == 0)
    def _():
        up_ref[sub_rows:sub_rows + carry, :] = jnp.zeros((carry, up_ref.shape[1]), _F32)
        for_tiles(step, True, start)

    for_tiles(step, True, wait)

    @pl.when(step + 1 < n_steps)
    def _():
        for_tiles(step + 1, True, start)

    @pl.when(step >= 2)
    def _():
        for_tiles(step - 2, False, wait)

    for sub in range(n_sub):
        slot = (step % 2) * n_sub + sub
        x = xin_ref[slot].reshape(sub_rows, d)
        h = (x * _rms_scale(x) * g_ref[...]).astype(_BF16)
        up_ref[0:carry, :] = up_ref[sub_rows:sub_rows + carry, :]

        def conv(lo, hi):
            up_ref[carry:carry + sub_rows, lo:hi] = jnp.dot(
                h, wup_ref[:, lo:hi], preferred_element_type=_F32)
            y = cb_ref[:, lo:hi]
            for tap in range(CONV_WIDTH):
                first = tap * batch
                y = y + up_ref[first:first + sub_rows, lo:hi] * cw_ref[tap:tap + 1, lo:hi]
            return y

        act = act_ref.at[sub % act_ref.shape[0]]
        for blk in range(D_FF // FF_BLOCK):
            lo, hi = blk * FF_BLOCK, (blk + 1) * FF_BLOCK
            a = conv(lo, hi)
            b = conv(D_FF + lo, D_FF + hi)
            act[:, lo:hi] = (a * jax.nn.sigmoid(a) * b).astype(_BF16)

        y = x + jnp.dot(act[...], wdown_ref[...], preferred_element_type=_F32)
        y = y * _rms_scale(y) * gf_ref[...]
        yout_ref[slot] = y.reshape(steps, batch, d)

    for_tiles(step, False, start)

    @pl.when(step == n_steps - 1)
    def _():
        for_tiles(step - 1, False, wait)
        for_tiles(step, False, wait)


def _const_spec(shape, grid_rank):
    zeros = (0,) * len(shape)
    index_map = (lambda j: zeros) if grid_rank == 1 else (lambda b, j: zeros)
    return pl.BlockSpec(shape, index_map, pipeline_mode=pl.Buffered(1))


def _as_f32_constants(tables):
    return tuple(jnp.asarray(t.astype(np.float32)) for t in tables)


def _decay_tables():
    log_gamma = np.log(1.0 - np.power(2.0, -5.0 - np.arange(RET_HEADS, dtype=np.float64)))
    pos = np.arange(RET_CHUNK, dtype=np.float64)
    diff = pos[:, None] - pos[None, :]
    mask = np.where(diff >= 0.0,
                    np.exp(log_gamma[:, None, None] * np.maximum(diff, 0.0)[None]), 0.0)
    k_decay = np.exp(log_gamma[:, None] * (RET_CHUNK - 1.0 - pos)[None])
    q_decay = np.exp(log_gamma[:, None] * (pos + 1.0)[None])
    chunk_decay = np.exp(log_gamma * RET_CHUNK)
    widen = lambda t: np.repeat(t.T, HEAD_DIM, axis=1)
    return _as_f32_constants(
        (mask, widen(k_decay), widen(q_decay), np.repeat(chunk_decay, HEAD_DIM)[None, :]))


def _rope_tables(seq):
    half = HEAD_DIM // 2
    inv_freq = np.power(ROPE_BASE, -np.arange(half, dtype=np.float64) / half)
    ang = np.arange(seq, dtype=np.float64)[:, None] * inv_freq[None, :]
    cos, sin = np.cos(ang), np.sin(ang)
    cos_full = np.concatenate([cos, cos], axis=-1)
    sin_signed = np.concatenate([-sin, sin], axis=-1)
    scale = HEAD_DIM ** -0.5
    return _as_f32_constants((cos_full, sin_signed, cos_full * scale, sin_signed * scale))


def kernel(x, mix_norm_g, w_in, ret_norm_g, sgu_ln_g, sgu_ln_b, sgu_w_s, sgu_b_s, w_out,
           ffn_norm_g, w_up, conv_w, conv_b, w_down, final_norm_g):
    batch, seq, d_model = x.shape
    assert w_in.shape[0] == 1
    assert d_model == D_MODEL and seq % MIX_ROWS == 0 and MIX_ROWS % CHUNK == 0
    assert batch % MIX_BATCH == 0 and seq % FFN_STEPS == 0 and FFN_STEPS % FFN_SUB_STEPS == 0
    assert D_FF % FF_BLOCK == 0
    l = 0

    cos, sin, cosk, sink = _rope_tables(seq)
    mask, kdec, qdec, cdec = _decay_tables()
    out_shape = jax.ShapeDtypeStruct(x.shape, _F32)

    mix_grid = (batch // MIX_BATCH, seq // MIX_ROWS)
    mix_steps = mix_grid[0] * mix_grid[1]
    assert D_MODEL % (16 * mix_steps) == 0 and D_FF % (16 * mix_steps) == 0
    mix_x_spec = pl.BlockSpec((MIX_BATCH, MIX_ROWS, D_MODEL), lambda b, j: (b, j, 0))
    rope_spec = pl.BlockSpec((MIX_ROWS, HEAD_DIM), lambda b, j: (j, 0))
    wup_spec = pl.BlockSpec((D_MODEL // mix_steps, 2 * D_FF),
                            lambda b, j: (b * mix_grid[1] + j, 0))
    wdown_spec = pl.BlockSpec((D_FF // mix_steps, D_MODEL),
                              lambda b, j: (b * mix_grid[1] + j, 0))
    c2 = lambda shape: _const_spec(shape, 2)
    hbm_spec = pl.BlockSpec(memory_space=pl.ANY)
    assert w_in.shape[1] % CAST_ROWS == 0 and w_out.shape[1] % CAST_ROWS == 0
    x, w_up_bf, w_down_bf = pl.pallas_call(
        _mix_kernel,
        grid=mix_grid,
        in_specs=[mix_x_spec, c2((1, D_MODEL)), hbm_spec,
                  rope_spec, rope_spec, rope_spec, rope_spec,
                  c2(mask.shape), c2(kdec.shape), c2(qdec.shape), c2(cdec.shape),
                  c2((1, RET_WIDTH)), c2(sgu_ln_g.shape[1:]), c2(sgu_ln_b.shape[1:]),
                  c2(sgu_w_s.shape[1:]), c2(sgu_b_s.shape[1:]), hbm_spec,
                  wup_spec, wdown_spec],
        out_specs=[mix_x_spec, wup_spec, wdown_spec],
        out_shape=[out_shape, jax.ShapeDtypeStruct(w_up.shape[1:], _BF16),
                   jax.ShapeDtypeStruct(w_down.shape[1:], _BF16)],
        scratch_shapes=[pltpu.VMEM((MIX_BATCH, RET_HEADS, HEAD_DIM, HEAD_DIM), _F32),
                        pltpu.VMEM((MIX_BATCH, MIX_ROWS, RET_WIDTH + SGU_WIDTH), _BF16),
                        pltpu.VMEM((SGU_GROUPS, CHUNK, CHUNK), _BF16),
                        pltpu.VMEM((CHUNK, SGU_WIDTH), _F32),
                        pltpu.VMEM((2, MIX_ROWS, w_in.shape[2]), _F32),
                        pltpu.VMEM(w_in.shape[1:], _BF16),
                        pltpu.VMEM(w_out.shape[1:], _BF16),
                        pltpu.VMEM((CAST_SLOTS, CAST_ROWS, w_in.shape[2]), _F32),
                        pltpu.VMEM((CAST_SLOTS, CAST_ROWS, w_out.shape[2]), _F32),
                        pltpu.SemaphoreType.DMA((CAST_SLOTS,)),
                        pltpu.SemaphoreType.DMA((CAST_SLOTS,))],
        compiler_params=pltpu.CompilerParams(dimension_semantics=("arbitrary", "arbitrary"),
                                             vmem_limit_bytes=VMEM_LIMIT_BYTES),
        name="token_mix",
    )(x, mix_norm_g[l][None, :], w_in[l], cos, sin, cosk, sink,
      mask, kdec, qdec, cdec, ret_norm_g[l][None, :],
      sgu_ln_g[l], sgu_ln_b[l],
      sgu_w_s[l], sgu_b_s[l], w_out[l], w_up[l], w_down[l])

    sub_rows = batch * FFN_SUB_STEPS
    subs_per_step = FFN_STEPS // FFN_SUB_STEPS
    assert seq // FFN_STEPS >= 2
    c1 = lambda shape: _const_spec(shape, 1)
    stage = pltpu.VMEM((2 * subs_per_step, FFN_SUB_STEPS, batch, D_MODEL), _F32)
    return pl.pallas_call(
        _ffn_kernel,
        grid=(seq // FFN_STEPS,),
        in_specs=[hbm_spec, c1((1, D_MODEL)), c1(w_up.shape[1:]),
                  c1(conv_w.shape[1:]), c1((1, 2 * D_FF)),
                  c1(w_down.shape[1:]), c1((1, D_MODEL))],
        out_specs=hbm_spec,
        out_shape=out_shape,
        scratch_shapes=[pltpu.VMEM((sub_rows + (CONV_WIDTH - 1) * batch, 2 * D_FF), _F32),
                        pltpu.VMEM((2, sub_rows, D_FF), _BF16),
                        stage, stage,
                        pltpu.SemaphoreType.DMA((2 * subs_per_step,)),
                        pltpu.SemaphoreType.DMA((2 * subs_per_step,))],
        compiler_params=pltpu.CompilerParams(dimension_semantics=("arbitrary",),
                                             vmem_limit_bytes=VMEM_LIMIT_BYTES),
        name="channel_mix",
    )(x, ffn_norm_g[l][None, :], w_up_bf, conv_w[l], conv_b[l][None, :],
      w_down_bf, final_norm_g[None, :])
```

```python
import jax
import jax.numpy as jnp
import numpy as np
from jax import lax
from jax.experimental import pallas as pl
from jax.experimental.pallas import tpu as pltpu

D_MODEL = 1024
RET_HEADS = 4
HEAD_DIM = 128
RET_WIDTH = RET_HEADS * HEAD_DIM
SGU_GROUPS = 4
GROUP_DIM = 128
SGU_WIDTH = SGU_GROUPS * GROUP_DIM
CHUNK = 128
ROPE_BASE = 10000.0
D_FF = 2816
CONV_WIDTH = 3
EPS = 1e-6

MIX_ROWS = 256
RET_CHUNK = 128
MIX_BATCH = 4
FFN_STEPS = 64
FFN_SUB_STEPS = 32
FF_BLOCK = 256
VMEM_LIMIT_BYTES = 56 * 1024 * 1024

CAST_ROWS = 128
CAST_SLOTS = 4
OUT_PROJ_DELAY = 2
BF16_ROWS = 16

_F32 = jnp.float32
_BF16 = jnp.bfloat16
_SQRT_HALF = float(np.sqrt(0.5))


def _rms_scale(x):
    return lax.rsqrt(jnp.mean(x * x, axis=-1, keepdims=True) + EPS)


def _gelu(x):
    return 0.5 * x * (1.0 + lax.erf(x * _SQRT_HALF))


def _rotary(t, cos_full, sin_signed):
    return t * cos_full + pltpu.roll(t, HEAD_DIM // 2, axis=1) * sin_signed


def _cast_rows_to_bf16(src_hbm, dst_ref, stage_ref, sems):
    n_slots, rows, _ = stage_ref.shape
    n_chunks = src_hbm.shape[0] // rows

    def chunk_copy(c):
        return pltpu.make_async_copy(src_hbm.at[pl.ds(c * rows, rows), :],
                                     stage_ref.at[c % n_slots], sems.at[c % n_slots])

    for c in range(min(n_slots - 1, n_chunks)):
        chunk_copy(c).start()
    for c in range(n_chunks):
        if c + n_slots - 1 < n_chunks:
            chunk_copy(c + n_slots - 1).start()
        chunk_copy(c).wait()
        dst_ref[c * rows:(c + 1) * rows, :] = stage_ref[c % n_slots].astype(_BF16)


def _mix_kernel(x_ref, g_ref, win_hbm, cos_ref, sin_ref, cosk_ref, sink_ref,
                mask_ref, kdec_ref, qdec_ref, cdec_ref, retg_ref, lng_ref, lnb_ref,
                wsf_ref, bsr_ref, wout_hbm, wup_ref, wdown_ref,
                o_ref, wup_bf_ref, wdown_bf_ref, state_ref, mix_ref, ws_ref, bs_ref,
                proj_ref, win_ref, wout_ref, win_stage, wout_stage, win_sems, wout_sems):
    n_sub, rows, _ = x_ref.shape

    def cast_up(lo):
        wup_bf_ref[lo:lo + BF16_ROWS, :] = wup_ref[lo:lo + BF16_ROWS, :].astype(_BF16)

    def cast_down():
        wdown_bf_ref[...] = wdown_ref[...].astype(_BF16)

    cast_units = [lambda lo=lo: cast_up(lo) for lo in range(0, wup_ref.shape[0], BF16_ROWS)]
    cast_units.append(cast_down)

    @pl.when(pl.program_id(1) == 0)
    def _():
        state_ref[...] = jnp.zeros_like(state_ref)

    @pl.when(jnp.logical_and(pl.program_id(0) == 0, pl.program_id(1) == 0))
    def _():
        _cast_rows_to_bf16(win_hbm, win_ref, win_stage, win_sems)
        _cast_rows_to_bf16(wout_hbm, wout_ref, wout_stage, wout_sems)
        row = lax.broadcasted_iota(jnp.int32, (CHUNK, CHUNK), 0)
        col = lax.broadcasted_iota(jnp.int32, (CHUNK, CHUNK), 1)
        causal = (row >= col).astype(_F32)
        for gr in range(SGU_GROUPS):
            ws_ref[gr] = (wsf_ref[gr] * causal).astype(_BF16)
            bias_row = jnp.broadcast_to(bsr_ref[gr:gr + 1, :], (CHUNK, CHUNK))
            bias_col = jnp.sum(jnp.where(row == col, bias_row, 0.0), axis=-1, keepdims=True)
            bs_ref[:, gr * GROUP_DIM:(gr + 1) * GROUP_DIM] = jnp.broadcast_to(
                bias_col, (CHUNK, GROUP_DIM))

    cos, sin = cos_ref[...], sin_ref[...]
    cosk, sink = cosk_ref[...], sink_ref[...]

    def project(i):
        pb = proj_ref.at[i % proj_ref.shape[0]]
        cache = {}

        def unit(lo):
            if not cache:
                x = x_ref[i]
                cache["h"] = (x * _rms_scale(x) * g_ref[...]).astype(_BF16)
            pb[:, lo:lo + RET_WIDTH] = jnp.dot(cache["h"], win_ref[:, lo:lo + RET_WIDTH],
                                               preferred_element_type=_F32)

        return [lambda lo=lo: unit(lo) for lo in range(0, win_ref.shape[1], RET_WIDTH)]

    def mix(i):
        pb = proj_ref.at[i % proj_ref.shape[0]]
        q = pb.at[:, 0:RET_WIDTH]
        k = pb.at[:, RET_WIDTH:2 * RET_WIDTH]
        v = pb.at[:, 2 * RET_WIDTH:3 * RET_WIDTH]
        g = pb.at[:, 3 * RET_WIDTH:4 * RET_WIDTH]
        u = pb.at[:, 4 * RET_WIDTH:4 * RET_WIDTH + SGU_WIDTH]
        sv = pb.at[:, 4 * RET_WIDTH + SGU_WIDTH:4 * RET_WIDTH + 2 * SGU_WIDTH]

        def head(c, hd):
            r = slice(c * RET_CHUNK, (c + 1) * RET_CHUNK)
            l = slice(hd * HEAD_DIM, (hd + 1) * HEAD_DIM)
            qh = _rotary(q[r, l], cos[r], sin[r])
            kh = _rotary(k[r, l], cosk[r], sink[r])
            vh = v[r, l].astype(_BF16)
            scores = lax.dot_general(qh.astype(_BF16), kh.astype(_BF16),
                                     (((1,), (1,)), ((), ())),
                                     preferred_element_type=_F32) * mask_ref[hd]
            intra = jnp.dot(scores.astype(_BF16), vh, preferred_element_type=_F32)
            state = state_ref[i, hd]
            cross = jnp.dot((qh * qdec_ref[:, l]).astype(_BF16), state.astype(_BF16),
                            preferred_element_type=_F32)
            kv = lax.dot_general((kh * kdec_ref[:, l]).astype(_BF16), vh,
                                 (((0,), (0,)), ((), ())), preferred_element_type=_F32)
            state_ref[i, hd] = state * cdec_ref[:, l] + kv
            ret = intra + cross
            ret = ret * _rms_scale(ret) * retg_ref[:, l]
            gh = g[r, l]
            mix_ref[i, r, l] = (gh * jax.nn.sigmoid(gh) * ret).astype(_BF16)

        def group(c, gr):
            r = slice(c * CHUNK, (c + 1) * CHUNK)
            l = slice(gr * GROUP_DIM, (gr + 1) * GROUP_DIM)
            gv = _gelu(sv[r, l])
            xc = gv - jnp.mean(gv, axis=-1, keepdims=True)
            vn = xc * _rms_scale(xc) * lng_ref[gr:gr + 1, :] + lnb_ref[gr:gr + 1, :]
            mixed = jnp.dot(ws_ref[gr], vn.astype(_BF16), preferred_element_type=_F32)
            out = _gelu(u[r, l]) * (mixed + bs_ref[:, l])
            mix_ref[i, r, RET_WIDTH + gr * GROUP_DIM:RET_WIDTH + (gr + 1) * GROUP_DIM] = (
                out.astype(_BF16))

        def out_proj():
            o_ref[i] = x_ref[i] + jnp.dot(mix_ref[i], wout_ref[...],
                                          preferred_element_type=_F32)

        units = []
        for c in range(rows // CHUNK):
            units += [lambda c=c, hd=hd: head(c, hd) for hd in range(RET_HEADS)]
            units += [lambda c=c, gr=gr: group(c, gr) for gr in range(SGU_GROUPS)]
        return units + [out_proj]

    for unit in project(0):
        unit()
    pending_out = None
    for i in range(n_sub):
        *mix_units, out_proj = mix(i)
        if pending_out is not None:
            mix_units.insert(OUT_PROJ_DELAY, pending_out)
        pending_out = out_proj
        if i + 1 == n_sub:
            mix_units.append(pending_out)
        proj_units = project(i + 1) if i + 1 < n_sub else cast_units
        stride = -(-len(mix_units) // (len(proj_units) + 1))
        for n, unit in enumerate(mix_units):
            unit()
            if proj_units and (n + 1) % stride == 0:
                proj_units.pop(0)()
        for unit in proj_units:
            unit()


def _ffn_kernel(x_hbm, g_ref, wup_ref, cw_ref, cb_ref, wdown_ref, gf_ref, o_hbm,
                up_ref, act_ref, xin_ref, yout_ref, in_sems, out_sems):
    _, steps, batch, d = xin_ref.shape
    sub_rows = batch * FFN_SUB_STEPS
    carry = (CONV_WIDTH - 1) * batch
    step = pl.program_id(0)
    last_step = pl.num_programs(0) - 1
    slot, other = step % 2, (step + 1) % 2

    def copies(tile, slot, incoming):
        t = pl.ds(tile * steps, steps)
        if incoming:
            return [pltpu.make_async_copy(x_hbm.at[b, t, :], xin_ref.at[slot, :, b, :],
                                          in_sems.at[slot]) for b in range(batch)]
        return [pltpu.make_async_copy(yout_ref.at[slot, :, b, :], o_hbm.at[b, t, :],
                                      out_sems.at[slot]) for b in range(batch)]

    @pl.when(step == 0)
    def _():
        up_ref[sub_rows:sub_rows + carry, :] = jnp.zeros((carry, up_ref.shape[1]), _F32)
        for copy in copies(step, slot, True):
            copy.start()

    for copy in copies(step, slot, True):
        copy.wait()
    for copy in copies(jnp.minimum(step + 1, last_step), other, True):
        copy.start()

    @pl.when(step >= 2)
    def _():
        for copy in copies(step - 2, slot, False):
            copy.wait()

    for sub in range(steps // FFN_SUB_STEPS):
        t = slice(sub * FFN_SUB_STEPS, (sub + 1) * FFN_SUB_STEPS)
        x = xin_ref[slot, t].reshape(sub_rows, d)
        h = (x * _rms_scale(x) * g_ref[...]).astype(_BF16)
        up_ref[0:carry, :] = up_ref[sub_rows:sub_rows + carry, :]

        def conv(lo, hi):
            up_ref[carry:carry + sub_rows, lo:hi] = jnp.dot(
                h, wup_ref[:, lo:hi], preferred_element_type=_F32)
            y = cb_ref[:, lo:hi]
            for tap in range(CONV_WIDTH):
                first = tap * batch
                y = y + up_ref[first:first + sub_rows, lo:hi] * cw_ref[tap:tap + 1, lo:hi]
            return y

        act = act_ref.at[sub % act_ref.shape[0]]
        for blk in range(D_FF // FF_BLOCK):
            lo, hi = blk * FF_BLOCK, (blk + 1) * FF_BLOCK
            a = conv(lo, hi)
            b = conv(D_FF + lo, D_FF + hi)
            act[:, lo:hi] = (a * jax.nn.sigmoid(a) * b).astype(_BF16)

        y = x + jnp.dot(act[...], wdown_ref[...], preferred_element_type=_F32)
        y = y * _rms_scale(y) * gf_ref[...]
        yout_ref[slot, t] = y.reshape(FFN_SUB_STEPS, batch, d)

    for copy in copies(step, slot, False):
        copy.start()

    @pl.when(step == last_step)
    def _():
        for copy in (copies(step - 1, other, False) + copies(step, slot, False)
                     + copies(step, other, True)):
            copy.wait()


def _const_spec(shape, grid_rank):
    zeros = (0,) * len(shape)
    index_map = (lambda j: zeros) if grid_rank == 1 else (lambda b, j: zeros)
    return pl.BlockSpec(shape, index_map, pipeline_mode=pl.Buffered(1))


def _as_f32_constants(tables):
    return tuple(jnp.asarray(t.astype(np.float32)) for t in tables)


def _decay_tables():
    log_gamma = np.log(1.0 - np.power(2.0, -5.0 - np.arange(RET_HEADS, dtype=np.float64)))
    pos = np.arange(RET_CHUNK, dtype=np.float64)
    diff = pos[:, None] - pos[None, :]
    mask = np.where(diff >= 0.0,
                    np.exp(log_gamma[:, None, None] * np.maximum(diff, 0.0)[None]), 0.0)
    k_decay = np.exp(log_gamma[:, None] * (RET_CHUNK - 1.0 - pos)[None])
    q_decay = np.exp(log_gamma[:, None] * (pos + 1.0)[None])
    chunk_decay = np.exp(log_gamma * RET_CHUNK)
    widen = lambda t: np.repeat(t.T, HEAD_DIM, axis=1)
    return _as_f32_constants(
        (mask, widen(k_decay), widen(q_decay), np.repeat(chunk_decay, HEAD_DIM)[None, :]))


def _rope_tables(seq):
    half = HEAD_DIM // 2
    inv_freq = np.power(ROPE_BASE, -np.arange(half, dtype=np.float64) / half)
    ang = np.arange(seq, dtype=np.float64)[:, None] * inv_freq[None, :]
    cos, sin = np.cos(ang), np.sin(ang)
    cos_full = np.concatenate([cos, cos], axis=-1)
    sin_signed = np.concatenate([-sin, sin], axis=-1)
    scale = HEAD_DIM ** -0.5
    return _as_f32_constants((cos_full, sin_signed, cos_full * scale, sin_signed * scale))


def kernel(x, mix_norm_g, w_in, ret_norm_g, sgu_ln_g, sgu_ln_b, sgu_w_s, sgu_b_s, w_out,
           ffn_norm_g, w_up, conv_w, conv_b, w_down, final_norm_g):
    batch, seq, d_model = x.shape
    assert w_in.shape[0] == 1
    assert d_model == D_MODEL and seq % MIX_ROWS == 0 and MIX_ROWS % CHUNK == 0
    assert batch % MIX_BATCH == 0 and seq % FFN_STEPS == 0 and FFN_STEPS % FFN_SUB_STEPS == 0
    assert D_FF % FF_BLOCK == 0
    l = 0

    cos, sin, cosk, sink = _rope_tables(seq)
    mask, kdec, qdec, cdec = _decay_tables()
    out_shape = jax.ShapeDtypeStruct(x.shape, _F32)

    mix_grid = (batch // MIX_BATCH, seq // MIX_ROWS)
    mix_steps = mix_grid[0] * mix_grid[1]
    assert D_MODEL % (16 * mix_steps) == 0 and D_FF % (16 * mix_steps) == 0
    mix_x_spec = pl.BlockSpec((MIX_BATCH, MIX_ROWS, D_MODEL), lambda b, j: (b, j, 0))
    rope_spec = pl.BlockSpec((MIX_ROWS, HEAD_DIM), lambda b, j: (j, 0))
    wup_spec = pl.BlockSpec((D_MODEL // mix_steps, 2 * D_FF),
                            lambda b, j: (b * mix_grid[1] + j, 0))
    wdown_spec = pl.BlockSpec((D_FF // mix_steps, D_MODEL),
                              lambda b, j: (b * mix_grid[1] + j, 0))
    c2 = lambda shape: _const_spec(shape, 2)
    hbm_spec = pl.BlockSpec(memory_space=pl.ANY)
    assert w_in.shape[1] % CAST_ROWS == 0 and w_out.shape[1] % CAST_ROWS == 0
    x, w_up_bf, w_down_bf = pl.pallas_call(
        _mix_kernel,
        grid=mix_grid,
        in_specs=[mix_x_spec, c2((1, D_MODEL)), hbm_spec,
                  rope_spec, rope_spec, rope_spec, rope_spec,
                  c2(mask.shape), c2(kdec.shape), c2(qdec.shape), c2(cdec.shape),
                  c2((1, RET_WIDTH)), c2(sgu_ln_g.shape[1:]), c2(sgu_ln_b.shape[1:]),
                  c2(sgu_w_s.shape[1:]), c2(sgu_b_s.shape[1:]), hbm_spec,
                  wup_spec, wdown_spec],
        out_specs=[mix_x_spec, wup_spec, wdown_spec],
        out_shape=[out_shape, jax.ShapeDtypeStruct(w_up.shape[1:], _BF16),
                   jax.ShapeDtypeStruct(w_down.shape[1:], _BF16)],
        scratch_shapes=[pltpu.VMEM((MIX_BATCH, RET_HEADS, HEAD_DIM, HEAD_DIM), _F32),
                        pltpu.VMEM((MIX_BATCH, MIX_ROWS, RET_WIDTH + SGU_WIDTH), _BF16),
                        pltpu.VMEM((SGU_GROUPS, CHUNK, CHUNK), _BF16),
                        pltpu.VMEM((CHUNK, SGU_WIDTH), _F32),
                        pltpu.VMEM((2, MIX_ROWS, w_in.shape[2]), _F32),
                        pltpu.VMEM(w_in.shape[1:], _BF16),
                        pltpu.VMEM(w_out.shape[1:], _BF16),
                        pltpu.VMEM((CAST_SLOTS, CAST_ROWS, w_in.shape[2]), _F32),
                        pltpu.VMEM((CAST_SLOTS, CAST_ROWS, w_out.shape[2]), _F32),
                        pltpu.SemaphoreType.DMA((CAST_SLOTS,)),
                        pltpu.SemaphoreType.DMA((CAST_SLOTS,))],
        compiler_params=pltpu.CompilerParams(dimension_semantics=("arbitrary", "arbitrary"),
                                             vmem_limit_bytes=VMEM_LIMIT_BYTES),
        name="token_mix",
    )(x, mix_norm_g[l][None, :], w_in[l], cos, sin, cosk, sink,
      mask, kdec, qdec, cdec, ret_norm_g[l][None, :],
      sgu_ln_g[l], sgu_ln_b[l],
      sgu_w_s[l], sgu_b_s[l], w_out[l], w_up[l], w_down[l])

    sub_rows = batch * FFN_SUB_STEPS
    assert seq // FFN_STEPS >= 2
    c1 = lambda shape: _const_spec(shape, 1)
    stage = pltpu.VMEM((2, FFN_STEPS, batch, D_MODEL), _F32)
    return pl.pallas_call(
        _ffn_kernel,
        grid=(seq // FFN_STEPS,),
        in_specs=[hbm_spec, c1((1, D_MODEL)), c1(w_up.shape[1:]),
                  c1(conv_w.shape[1:]), c1((1, 2 * D_FF)),
                  c1(w_down.shape[1:]), c1((1, D_MODEL))],
        out_specs=hbm_spec,
        out_shape=out_shape,
        scratch_shapes=[pltpu.VMEM((sub_rows + (CONV_WIDTH - 1) * batch, 2 * D_FF), _F32),
                        pltpu.VMEM((2, sub_rows, D_FF), _BF16),
                        stage, stage,
                        pltpu.SemaphoreType.DMA((2,)),
                        pltpu.SemaphoreType.DMA((2,))],
        compiler_params=pltpu.CompilerParams(dimension_semantics=("arbitrary",),
                                             vmem_limit_bytes=VMEM_LIMIT_BYTES),
        name="channel_mix",
    )(x, ffn_norm_g[l][None, :], w_up_bf, conv_w[l], conv_b[l][None, :],
      w_down_bf, final_norm_g[None, :])
```

```python
import jax
import jax.numpy as jnp
import numpy as np
from jax import lax
from jax.experimental import pallas as pl
from jax.experimental.pallas import tpu as pltpu

D_MODEL = 1024
RET_HEADS = 4
HEAD_DIM = 128
RET_WIDTH = RET_HEADS * HEAD_DIM
SGU_GROUPS = 4
GROUP_DIM = 128
SGU_WIDTH = SGU_GROUPS * GROUP_DIM
CHUNK = 128
ROPE_BASE = 10000.0
D_FF = 2816
CONV_WIDTH = 3
EPS = 1e-6

MIX_ROWS = 256
RET_CHUNK = 128
MIX_BATCH = 4
FFN_STEPS = 64
FFN_SUB_STEPS = 32
FF_BLOCK = 256
VMEM_LIMIT_BYTES = 56 * 1024 * 1024

CAST_ROWS = 128
CAST_SLOTS = 4
OUT_PROJ_DELAY = 2
BF16_ROWS = 16

_F32 = jnp.float32
_BF16 = jnp.bfloat16
_SQRT_HALF = float(np.sqrt(0.5))


def _rms_scale(x):
    return lax.rsqrt(jnp.mean(x * x, axis=-1, keepdims=True) + EPS)


def _gelu(x):
    return 0.5 * x * (1.0 + lax.erf(x * _SQRT_HALF))


def _rotary(t, cos_full, sin_signed):
    return t * cos_full + pltpu.roll(t, HEAD_DIM // 2, axis=1) * sin_signed


def _cast_rows_to_bf16(src_hbm, dst_ref, stage_ref, sems):
    n_slots, rows, _ = stage_ref.shape
    n_chunks = src_hbm.shape[0] // rows

    def chunk_copy(c):
        return pltpu.make_async_copy(src_hbm.at[pl.ds(c * rows, rows), :],
                                     stage_ref.at[c % n_slots], sems.at[c % n_slots])

    for c in range(min(n_slots - 1, n_chunks)):
        chunk_copy(c).start()
    for c in range(n_chunks):
        if c + n_slots - 1 < n_chunks:
            chunk_copy(c + n_slots - 1).start()
        chunk_copy(c).wait()
        dst_ref[c * rows:(c + 1) * rows, :] = stage_ref[c % n_slots].astype(_BF16)


def _mix_kernel(x_ref, g_ref, win_hbm, cos_ref, sin_ref, cosk_ref, sink_ref,
                mask_ref, kdec_ref, qdec_ref, cdec_ref, retg_ref, lng_ref, lnb_ref,
                wsf_ref, bsr_ref, wout_hbm, wup_ref, wdown_ref,
                o_ref, wup_bf_ref, wdown_bf_ref, state_ref, mix_ref, ws_ref, bs_ref,
                proj_ref, win_ref, wout_ref, win_stage, wout_stage, win_sems, wout_sems):
    n_sub, rows, _ = x_ref.shape

    def cast_up(lo):
        wup_bf_ref[lo:lo + BF16_ROWS, :] = wup_ref[lo:lo + BF16_ROWS, :].astype(_BF16)

    def cast_down():
        wdown_bf_ref[...] = wdown_ref[...].astype(_BF16)

    cast_units = [lambda lo=lo: cast_up(lo) for lo in range(0, wup_ref.shape[0], BF16_ROWS)]
    cast_units.append(cast_down)

    @pl.when(pl.program_id(1) == 0)
    def _():
        state_ref[...] = jnp.zeros_like(state_ref)

    @pl.when(jnp.logical_and(pl.program_id(0) == 0, pl.program_id(1) == 0))
    def _():
        _cast_rows_to_bf16(win_hbm, win_ref, win_stage, win_sems)
        _cast_rows_to_bf16(wout_hbm, wout_ref, wout_stage, wout_sems)
        row = lax.broadcasted_iota(jnp.int32, (CHUNK, CHUNK), 0)
        col = lax.broadcasted_iota(jnp.int32, (CHUNK, CHUNK), 1)
        causal = (row >= col).astype(_F32)
        for gr in range(SGU_GROUPS):
            ws_ref[gr] = (wsf_ref[gr] * causal).astype(_BF16)
            bias_row = jnp.broadcast_to(bsr_ref[gr:gr + 1, :], (CHUNK, CHUNK))
            bias_col = jnp.sum(jnp.where(row == col, bias_row, 0.0), axis=-1, keepdims=True)
            bs_ref[:, gr * GROUP_DIM:(gr + 1) * GROUP_DIM] = jnp.broadcast_to(
                bias_col, (CHUNK, GROUP_DIM))

    cos, sin = cos_ref[...], sin_ref[...]
    cosk, sink = cosk_ref[...], sink_ref[...]

    def project(i):
        pb = proj_ref.at[i % proj_ref.shape[0]]
        cache = {}

        def unit(lo):
            if not cache:
                x = x_ref[i]
                cache["h"] = (x * _rms_scale(x) * g_ref[...]).astype(_BF16)
            pb[:, lo:lo + RET_WIDTH] = jnp.dot(cache["h"], win_ref[:, lo:lo + RET_WIDTH],
                                               preferred_element_type=_F32)

        return [lambda lo=lo: unit(lo) for lo in range(0, win_ref.shape[1], RET_WIDTH)]

    def mix(i):
        pb = proj_ref.at[i % proj_ref.shape[0]]
        q = pb.at[:, 0:RET_WIDTH]
        k = pb.at[:, RET_WIDTH:2 * RET_WIDTH]
        v = pb.at[:, 2 * RET_WIDTH:3 * RET_WIDTH]
        g = pb.at[:, 3 * RET_WIDTH:4 * RET_WIDTH]
        u = pb.at[:, 4 * RET_WIDTH:4 * RET_WIDTH + SGU_WIDTH]
        sv = pb.at[:, 4 * RET_WIDTH + SGU_WIDTH:4 * RET_WIDTH + 2 * SGU_WIDTH]

        held = {}

        def retention_first(c):
            r = slice(c * RET_CHUNK, (c + 1) * RET_CHUNK)
            for hd in range(RET_HEADS):
                l = slice(hd * HEAD_DIM, (hd + 1) * HEAD_DIM)
                qh = _rotary(q[r, l], cos[r], sin[r])
                kh = _rotary(k[r, l], cosk[r], sink[r])
                vh = v[r, l].astype(_BF16)
                held["scores", c, hd] = lax.dot_general(
                    qh.astype(_BF16), kh.astype(_BF16), (((1,), (1,)), ((), ())),
                    preferred_element_type=_F32)
                state = state_ref[i, hd]
                held["cross", c, hd] = jnp.dot(
                    (qh * qdec_ref[:, l]).astype(_BF16), state.astype(_BF16),
                    preferred_element_type=_F32)
                kv = lax.dot_general((kh * kdec_ref[:, l]).astype(_BF16), vh,
                                     (((0,), (0,)), ((), ())), preferred_element_type=_F32)
                state_ref[i, hd] = state * cdec_ref[:, l] + kv
                held["v", c, hd] = vh

        def retention_second(c):
            r = slice(c * RET_CHUNK, (c + 1) * RET_CHUNK)
            for hd in range(RET_HEADS):
                l = slice(hd * HEAD_DIM, (hd + 1) * HEAD_DIM)
                scores = held.pop(("scores", c, hd)) * mask_ref[hd]
                intra = jnp.dot(scores.astype(_BF16), held.pop(("v", c, hd)),
                                preferred_element_type=_F32)
                ret = intra + held.pop(("cross", c, hd))
                ret = ret * _rms_scale(ret) * retg_ref[:, l]
                gh = g[r, l]
                mix_ref[i, r, l] = (gh * jax.nn.sigmoid(gh) * ret).astype(_BF16)

        def gating_first(c):
            r = slice(c * CHUNK, (c + 1) * CHUNK)
            for gr in range(SGU_GROUPS):
                l = slice(gr * GROUP_DIM, (gr + 1) * GROUP_DIM)
                gv = _gelu(sv[r, l])
                xc = gv - jnp.mean(gv, axis=-1, keepdims=True)
                vn = xc * _rms_scale(xc) * lng_ref[gr:gr + 1, :] + lnb_ref[gr:gr + 1, :]
                held["mixed", c, gr] = jnp.dot(ws_ref[gr], vn.astype(_BF16),
                                               preferred_element_type=_F32)

        def gating_second(c):
            r = slice(c * CHUNK, (c + 1) * CHUNK)
            for gr in range(SGU_GROUPS):
                l = slice(gr * GROUP_DIM, (gr + 1) * GROUP_DIM)
                out = _gelu(u[r, l]) * (held.pop(("mixed", c, gr)) + bs_ref[:, l])
                mix_ref[i, r, RET_WIDTH + gr * GROUP_DIM:RET_WIDTH + (gr + 1) * GROUP_DIM] = (
                    out.astype(_BF16))

        def out_proj():
            o_ref[i] = x_ref[i] + jnp.dot(mix_ref[i], wout_ref[...],
                                          preferred_element_type=_F32)

        assert RET_CHUNK == CHUNK
        units = []
        for c in range(rows // CHUNK):
            units += [lambda c=c: retention_first(c), lambda c=c: gating_first(c),
                      lambda c=c: retention_second(c), lambda c=c: gating_second(c)]
        return units + [out_proj]

    for unit in project(0):
        unit()
    pending_out = None
    for i in range(n_sub):
        *mix_units, out_proj = mix(i)
        if pending_out is not None:
            mix_units.insert(OUT_PROJ_DELAY, pending_out)
        pending_out = out_proj
        if i + 1 == n_sub:
            mix_units.append(pending_out)
        proj_units = project(i + 1) if i + 1 < n_sub else cast_units
        stride = max(1, len(mix_units) // len(proj_units))
        for n, unit in enumerate(mix_units):
            unit()
            if proj_units and (n + 1) % stride == 0:
                proj_units.pop(0)()
        for unit in proj_units:
            unit()


def _ffn_kernel(x_hbm, g_ref, wup_ref, cw_ref, cb_ref, wdown_ref, gf_ref, o_hbm,
                up_ref, act_ref, xin_ref, yout_ref, in_sems, out_sems):
    _, steps, batch, d = xin_ref.shape
    sub_rows = batch * FFN_SUB_STEPS
    carry = (CONV_WIDTH - 1) * batch
    step = pl.program_id(0)
    last_step = pl.num_programs(0) - 1
    slot, other = step % 2, (step + 1) % 2

    def copies(tile, slot, incoming):
        t = pl.ds(tile * steps, steps)
        if incoming:
            return [pltpu.make_async_copy(x_hbm.at[b, t, :], xin_ref.at[slot, :, b, :],
                                          in_sems.at[slot]) for b in range(batch)]
        return [pltpu.make_async_copy(yout_ref.at[slot, :, b, :], o_hbm.at[b, t, :],
                                      out_sems.at[slot]) for b in range(batch)]

    @pl.when(step == 0)
    def _():
        up_ref[sub_rows:sub_rows + carry, :] = jnp.zeros((carry, up_ref.shape[1]), _F32)
        for copy in copies(step, slot, True):
            copy.start()

    for copy in copies(step, slot, True):
        copy.wait()
    for copy in copies(jnp.minimum(step + 1, last_step), other, True):
        copy.start()

    @pl.when(step >= 2)
    def _():
        for copy in copies(step - 2, slot, False):
            copy.wait()

    for sub in range(steps // FFN_SUB_STEPS):
        t = slice(sub * FFN_SUB_STEPS, (sub + 1) * FFN_SUB_STEPS)
        x = xin_ref[slot, t].reshape(sub_rows, d)
        h = (x * _rms_scale(x) * g_ref[...]).astype(_BF16)
        up_ref[0:carry, :] = up_ref[sub_rows:sub_rows + carry, :]

        def conv(lo, hi):
            up_ref[carry:carry + sub_rows, lo:hi] = jnp.dot(
                h, wup_ref[:, lo:hi], preferred_element_type=_F32)
            y = cb_ref[:, lo:hi]
            for tap in range(CONV_WIDTH):
                first = tap * batch
                y = y + up_ref[first:first + sub_rows, lo:hi] * cw_ref[tap:tap + 1, lo:hi]
            return y

        act = act_ref.at[sub % act_ref.shape[0]]
        for blk in range(D_FF // FF_BLOCK):
            lo, hi = blk * FF_BLOCK, (blk + 1) * FF_BLOCK
            a = conv(lo, hi)
            b = conv(D_FF + lo, D_FF + hi)
            act[:, lo:hi] = (a * jax.nn.sigmoid(a) * b).astype(_BF16)

        y = x + jnp.dot(act[...], wdown_ref[...], preferred_element_type=_F32)
        y = y * _rms_scale(y) * gf_ref[...]
        yout_ref[slot, t] = y.reshape(FFN_SUB_STEPS, batch, d)

    for copy in copies(step, slot, False):
        copy.start()

    @pl.when(step == last_step)
    def _():
        for copy in (copies(step - 1, other, False) + copies(step, slot, False)
                     + copies(step, other, True)):
            copy.wait()


def _const_spec(shape, grid_rank):
    zeros = (0,) * len(shape)
    index_map = (lambda j: zeros) if grid_rank == 1 else (lambda b, j: zeros)
    return pl.BlockSpec(shape, index_map, pipeline_mode=pl.Buffered(1))


def _as_f32_constants(tables):
    return tuple(jnp.asarray(t.astype(np.float32)) for t in tables)


def _decay_tables():
    log_gamma = np.log(1.0 - np.power(2.0, -5.0 - np.arange(RET_HEADS, dtype=np.float64)))
    pos = np.arange(RET_CHUNK, dtype=np.float64)
    diff = pos[:, None] - pos[None, :]
    mask = np.where(diff >= 0.0,
                    np.exp(log_gamma[:, None, None] * np.maximum(diff, 0.0)[None]), 0.0)
    k_decay = np.exp(log_gamma[:, None] * (RET_CHUNK - 1.0 - pos)[None])
    q_decay = np.exp(log_gamma[:, None] * (pos + 1.0)[None])
    chunk_decay = np.exp(log_gamma * RET_CHUNK)
    widen = lambda t: np.repeat(t.T, HEAD_DIM, axis=1)
    return _as_f32_constants(
        (mask, widen(k_decay), widen(q_decay), np.repeat(chunk_decay, HEAD_DIM)[None, :]))


def _rope_tables(seq):
    half = HEAD_DIM // 2
    inv_freq = np.power(ROPE_BASE, -np.arange(half, dtype=np.float64) / half)
    ang = np.arange(seq, dtype=np.float64)[:, None] * inv_freq[None, :]
    cos, sin = np.cos(ang), np.sin(ang)
    cos_full = np.concatenate([cos, cos], axis=-1)
    sin_signed = np.concatenate([-sin, sin], axis=-1)
    scale = HEAD_DIM ** -0.5
    return _as_f32_constants((cos_full, sin_signed, cos_full * scale, sin_signed * scale))


def kernel(x, mix_norm_g, w_in, ret_norm_g, sgu_ln_g, sgu_ln_b, sgu_w_s, sgu_b_s, w_out,
           ffn_norm_g, w_up, conv_w, conv_b, w_down, final_norm_g):
    batch, seq, d_model = x.shape
    assert w_in.shape[0] == 1
    assert d_model == D_MODEL and seq % MIX_ROWS == 0 and MIX_ROWS % CHUNK == 0
    assert batch % MIX_BATCH == 0 and seq % FFN_STEPS == 0 and FFN_STEPS % FFN_SUB_STEPS == 0
    assert D_FF % FF_BLOCK == 0
    l = 0

    cos, sin, cosk, sink = _rope_tables(seq)
    mask, kdec, qdec, cdec = _decay_tables()
    out_shape = jax.ShapeDtypeStruct(x.shape, _F32)

    mix_grid = (batch // MIX_BATCH, seq // MIX_ROWS)
    mix_steps = mix_grid[0] * mix_grid[1]
    assert D_MODEL % (16 * mix_steps) == 0 and D_FF % (16 * mix_steps) == 0
    mix_x_spec = pl.BlockSpec((MIX_BATCH, MIX_ROWS, D_MODEL), lambda b, j: (b, j, 0))
    rope_spec = pl.BlockSpec((MIX_ROWS, HEAD_DIM), lambda b, j: (j, 0))
    wup_spec = pl.BlockSpec((D_MODEL // mix_steps, 2 * D_FF),
                            lambda b, j: (b * mix_grid[1] + j, 0))
    wdown_spec = pl.BlockSpec((D_FF // mix_steps, D_MODEL),
                              lambda b, j: (b * mix_grid[1] + j, 0))
    c2 = lambda shape: _const_spec(shape, 2)
    hbm_spec = pl.BlockSpec(memory_space=pl.ANY)
    assert w_in.shape[1] % CAST_ROWS == 0 and w_out.shape[1] % CAST_ROWS == 0
    x, w_up_bf, w_down_bf = pl.pallas_call(
        _mix_kernel,
        grid=mix_grid,
        in_specs=[mix_x_spec, c2((1, D_MODEL)), hbm_spec,
                  rope_spec, rope_spec, rope_spec, rope_spec,
                  c2(mask.shape), c2(kdec.shape), c2(qdec.shape), c2(cdec.shape),
                  c2((1, RET_WIDTH)), c2(sgu_ln_g.shape[1:]), c2(sgu_ln_b.shape[1:]),
                  c2(sgu_w_s.shape[1:]), c2(sgu_b_s.shape[1:]), hbm_spec,
                  wup_spec, wdown_spec],
        out_specs=[mix_x_spec, wup_spec, wdown_spec],
        out_shape=[out_shape, jax.ShapeDtypeStruct(w_up.shape[1:], _BF16),
                   jax.ShapeDtypeStruct(w_down.shape[1:], _BF16)],
        scratch_shapes=[pltpu.VMEM((MIX_BATCH, RET_HEADS, HEAD_DIM, HEAD_DIM), _F32),
                        pltpu.VMEM((MIX_BATCH, MIX_ROWS, RET_WIDTH + SGU_WIDTH), _BF16),
                        pltpu.VMEM((SGU_GROUPS, CHUNK, CHUNK), _BF16),
                        pltpu.VMEM((CHUNK, SGU_WIDTH), _F32),
                        pltpu.VMEM((2, MIX_ROWS, w_in.shape[2]), _F32),
                        pltpu.VMEM(w_in.shape[1:], _BF16),
                        pltpu.VMEM(w_out.shape[1:], _BF16),
                        pltpu.VMEM((CAST_SLOTS, CAST_ROWS, w_in.shape[2]), _F32),
                        pltpu.VMEM((CAST_SLOTS, CAST_ROWS, w_out.shape[2]), _F32),
                        pltpu.SemaphoreType.DMA((CAST_SLOTS,)),
                        pltpu.SemaphoreType.DMA((CAST_SLOTS,))],
        compiler_params=pltpu.CompilerParams(dimension_semantics=("arbitrary", "arbitrary"),
                                             vmem_limit_bytes=VMEM_LIMIT_BYTES),
        name="token_mix",
    )(x, mix_norm_g[l][None, :], w_in[l], cos, sin, cosk, sink,
      mask, kdec, qdec, cdec, ret_norm_g[l][None, :],
      sgu_ln_g[l], sgu_ln_b[l],
      sgu_w_s[l], sgu_b_s[l], w_out[l], w_up[l], w_down[l])

    sub_rows = batch * FFN_SUB_STEPS
    assert seq // FFN_STEPS >= 2
    c1 = lambda shape: _const_spec(shape, 1)
    stage = pltpu.VMEM((2, FFN_STEPS, batch, D_MODEL), _F32)
    return pl.pallas_call(
        _ffn_kernel,
        grid=(seq // FFN_STEPS,),
        in_specs=[hbm_spec, c1((1, D_MODEL)), c1(w_up.shape[1:]),
                  c1(conv_w.shape[1:]), c1((1, 2 * D_FF)),
                  c1(w_down.shape[1:]), c1((1, D_MODEL))],
        out_specs=hbm_spec,
        out_shape=out_shape,
        scratch_shapes=[pltpu.VMEM((sub_rows + (CONV_WIDTH - 1) * batch, 2 * D_FF), _F32),
                        pltpu.VMEM((2, sub_rows, D_FF), _BF16),
                        stage, stage,
                        pltpu.SemaphoreType.DMA((2,)),
                        pltpu.SemaphoreType.DMA((2,))],
        compiler_params=pltpu.CompilerParams(dimension_semantics=("arbitrary",),
                                             vmem_limit_bytes=VMEM_LIMIT_BYTES),
        name="channel_mix",
    )(x, ffn_norm_g[l][None, :], w_up_bf, conv_w[l], conv_b[l][None, :],
      w_down_bf, final_norm_g[None, :])
```

```python
import jax
import jax.numpy as jnp
import numpy as np
from jax import lax
from jax.experimental import pallas as pl
from jax.experimental.pallas import tpu as pltpu

D_MODEL = 1024
RET_HEADS = 4
HEAD_DIM = 128
RET_WIDTH = RET_HEADS * HEAD_DIM
SGU_GROUPS = 4
GROUP_DIM = 128
SGU_WIDTH = SGU_GROUPS * GROUP_DIM
CHUNK = 128
ROPE_BASE = 10000.0
D_FF = 2816
CONV_WIDTH = 3
EPS = 1e-6

MIX_ROWS = 256
RET_CHUNK = 128
MIX_BATCH = 4
FFN_STEPS = 64
FFN_SUB_STEPS = 32
FF_BLOCK = 256
DOWN_DELAY = 2
DOWN_SPLIT = 2048
VMEM_LIMIT_BYTES = 56 * 1024 * 1024

CAST_ROWS = 128
CAST_SLOTS = 4
OUT_PROJ_DELAY = 2
BF16_ROWS = 16

_F32 = jnp.float32
_BF16 = jnp.bfloat16
_SQRT_HALF = float(np.sqrt(0.5))


def _rms_scale(x):
    return lax.rsqrt(jnp.mean(x * x, axis=-1, keepdims=True) + EPS)


def _gelu(x):
    return 0.5 * x * (1.0 + lax.erf(x * _SQRT_HALF))


def _rotary(t, cos_full, sin_signed):
    return t * cos_full + pltpu.roll(t, HEAD_DIM // 2, axis=1) * sin_signed


def _cast_rows_to_bf16(src_hbm, dst_ref, stage_ref, sems):
    n_slots, rows, _ = stage_ref.shape
    n_chunks = src_hbm.shape[0] // rows

    def chunk_copy(c):
        return pltpu.make_async_copy(src_hbm.at[pl.ds(c * rows, rows), :],
                                     stage_ref.at[c % n_slots], sems.at[c % n_slots])

    for c in range(min(n_slots - 1, n_chunks)):
        chunk_copy(c).start()
    for c in range(n_chunks):
        if c + n_slots - 1 < n_chunks:
            chunk_copy(c + n_slots - 1).start()
        chunk_copy(c).wait()
        dst_ref[c * rows:(c + 1) * rows, :] = stage_ref[c % n_slots].astype(_BF16)


def _mix_kernel(x_ref, g_ref, win_hbm, cos_ref, sin_ref, cosk_ref, sink_ref,
                mask_ref, kdec_ref, qdec_ref, cdec_ref, retg_ref, lng_ref, lnb_ref,
                wsf_ref, bsr_ref, wout_hbm, wup_ref, wdown_ref,
                o_ref, wup_bf_ref, wdown_bf_ref, state_ref, mix_ref, ws_ref, bs_ref,
                proj_ref, win_ref, wout_ref, win_stage, wout_stage, win_sems, wout_sems):
    n_sub, rows, _ = x_ref.shape

    def cast_up(lo):
        wup_bf_ref[lo:lo + BF16_ROWS, :] = wup_ref[lo:lo + BF16_ROWS, :].astype(_BF16)

    def cast_down():
        wdown_bf_ref[...] = wdown_ref[...].astype(_BF16)

    cast_units = [lambda lo=lo: cast_up(lo) for lo in range(0, wup_ref.shape[0], BF16_ROWS)]
    cast_units.append(cast_down)

    @pl.when(pl.program_id(1) == 0)
    def _():
        state_ref[...] = jnp.zeros_like(state_ref)

    @pl.when(jnp.logical_and(pl.program_id(0) == 0, pl.program_id(1) == 0))
    def _():
        _cast_rows_to_bf16(win_hbm, win_ref, win_stage, win_sems)
        _cast_rows_to_bf16(wout_hbm, wout_ref, wout_stage, wout_sems)
        row = lax.broadcasted_iota(jnp.int32, (CHUNK, CHUNK), 0)
        col = lax.broadcasted_iota(jnp.int32, (CHUNK, CHUNK), 1)
        causal = (row >= col).astype(_F32)
        for gr in range(SGU_GROUPS):
            ws_ref[gr] = (wsf_ref[gr] * causal).astype(_BF16)
            bias_row = jnp.broadcast_to(bsr_ref[gr:gr + 1, :], (CHUNK, CHUNK))
            bias_col = jnp.sum(jnp.where(row == col, bias_row, 0.0), axis=-1, keepdims=True)
            bs_ref[:, gr * GROUP_DIM:(gr + 1) * GROUP_DIM] = jnp.broadcast_to(
                bias_col, (CHUNK, GROUP_DIM))

    cos, sin = cos_ref[...], sin_ref[...]
    cosk, sink = cosk_ref[...], sink_ref[...]

    def project(i):
        pb = proj_ref.at[i % proj_ref.shape[0]]
        cache = {}

        def unit(lo):
            if not cache:
                x = x_ref[i]
                cache["h"] = (x * _rms_scale(x) * g_ref[...]).astype(_BF16)
            pb[:, lo:lo + RET_WIDTH] = jnp.dot(cache["h"], win_ref[:, lo:lo + RET_WIDTH],
                                               preferred_element_type=_F32)

        return [lambda lo=lo: unit(lo) for lo in range(0, win_ref.shape[1], RET_WIDTH)]

    def mix(i):
        pb = proj_ref.at[i % proj_ref.shape[0]]
        q = pb.at[:, 0:RET_WIDTH]
        k = pb.at[:, RET_WIDTH:2 * RET_WIDTH]
        v = pb.at[:, 2 * RET_WIDTH:3 * RET_WIDTH]
        g = pb.at[:, 3 * RET_WIDTH:4 * RET_WIDTH]
        u = pb.at[:, 4 * RET_WIDTH:4 * RET_WIDTH + SGU_WIDTH]
        sv = pb.at[:, 4 * RET_WIDTH + SGU_WIDTH:4 * RET_WIDTH + 2 * SGU_WIDTH]

        held = {}

        def retention_first(c):
            r = slice(c * RET_CHUNK, (c + 1) * RET_CHUNK)
            for hd in range(RET_HEADS):
                l = slice(hd * HEAD_DIM, (hd + 1) * HEAD_DIM)
                qh = _rotary(q[r, l], cos[r], sin[r])
                kh = _rotary(k[r, l], cosk[r], sink[r])
                vh = v[r, l].astype(_BF16)
                held["scores", c, hd] = lax.dot_general(
                    qh.astype(_BF16), kh.astype(_BF16), (((1,), (1,)), ((), ())),
                    preferred_element_type=_F32)
                state = state_ref[i, hd]
                held["cross", c, hd] = jnp.dot(
                    (qh * qdec_ref[:, l]).astype(_BF16), state.astype(_BF16),
                    preferred_element_type=_F32)
                kv = lax.dot_general((kh * kdec_ref[:, l]).astype(_BF16), vh,
                                     (((0,), (0,)), ((), ())), preferred_element_type=_F32)
                state_ref[i, hd] = state * cdec_ref[:, l] + kv
                held["v", c, hd] = vh

        def retention_second(c):
            r = slice(c * RET_CHUNK, (c + 1) * RET_CHUNK)
            for hd in range(RET_HEADS):
                l = slice(hd * HEAD_DIM, (hd + 1) * HEAD_DIM)
                scores = held.pop(("scores", c, hd)) * mask_ref[hd]
                intra = jnp.dot(scores.astype(_BF16), held.pop(("v", c, hd)),
                                preferred_element_type=_F32)
                ret = intra + held.pop(("cross", c, hd))
                ret = ret * _rms_scale(ret) * retg_ref[:, l]
                gh = g[r, l]
                mix_ref[i, r, l] = (gh * jax.nn.sigmoid(gh) * ret).astype(_BF16)

        def gating_first(c):
            r = slice(c * CHUNK, (c + 1) * CHUNK)
            for gr in range(SGU_GROUPS):
                l = slice(gr * GROUP_DIM, (gr + 1) * GROUP_DIM)
                gv = _gelu(sv[r, l])
                xc = gv - jnp.mean(gv, axis=-1, keepdims=True)
                vn = xc * _rms_scale(xc) * lng_ref[gr:gr + 1, :] + lnb_ref[gr:gr + 1, :]
                held["mixed", c, gr] = jnp.dot(ws_ref[gr], vn.astype(_BF16),
                                               preferred_element_type=_F32)

        def gating_second(c):
            r = slice(c * CHUNK, (c + 1) * CHUNK)
            for gr in range(SGU_GROUPS):
                l = slice(gr * GROUP_DIM, (gr + 1) * GROUP_DIM)
                out = _gelu(u[r, l]) * (held.pop(("mixed", c, gr)) + bs_ref[:, l])
                mix_ref[i, r, RET_WIDTH + gr * GROUP_DIM:RET_WIDTH + (gr + 1) * GROUP_DIM] = (
                    out.astype(_BF16))

        def out_proj():
            o_ref[i] = x_ref[i] + jnp.dot(mix_ref[i], wout_ref[...],
                                          preferred_element_type=_F32)

        assert RET_CHUNK == CHUNK
        units = []
        for c in range(rows // CHUNK):
            units += [lambda c=c: retention_first(c), lambda c=c: gating_first(c),
                      lambda c=c: retention_second(c), lambda c=c: gating_second(c)]
        return units + [out_proj]

    for unit in project(0):
        unit()
    pending_out = None
    for i in range(n_sub):
        *mix_units, out_proj = mix(i)
        if pending_out is not None:
            mix_units.insert(OUT_PROJ_DELAY, pending_out)
        pending_out = out_proj
        if i + 1 == n_sub:
            mix_units.append(pending_out)
        proj_units = project(i + 1) if i + 1 < n_sub else cast_units
        stride = max(1, len(mix_units) // len(proj_units))
        for n, unit in enumerate(mix_units):
            unit()
            if proj_units and (n + 1) % stride == 0:
                proj_units.pop(0)()
        for unit in proj_units:
            unit()


def _ffn_kernel(x_hbm, g_ref, wup_ref, cw_ref, cb_ref, wdown_ref, gf_ref, o_hbm,
                up_ref, act_ref, xin_ref, yout_ref, in_sems, out_sems):
    _, steps, batch, d = xin_ref.shape
    sub_rows = batch * FFN_SUB_STEPS
    carry = (CONV_WIDTH - 1) * batch
    step = pl.program_id(0)
    last_step = pl.num_programs(0) - 1
    slot, other = step % 2, (step + 1) % 2

    def copies(tile, slot, incoming):
        t = pl.ds(tile * steps, steps)
        if incoming:
            return [pltpu.make_async_copy(x_hbm.at[b, t, :], xin_ref.at[slot, :, b, :],
                                          in_sems.at[slot]) for b in range(batch)]
        return [pltpu.make_async_copy(yout_ref.at[slot, :, b, :], o_hbm.at[b, t, :],
                                      out_sems.at[slot]) for b in range(batch)]

    @pl.when(step == 0)
    def _():
        up_ref[sub_rows:sub_rows + carry, :] = jnp.zeros((carry, up_ref.shape[1]), _F32)
        for copy in copies(step, slot, True):
            copy.start()

    for copy in copies(step, slot, True):
        copy.wait()
    for copy in copies(jnp.minimum(step + 1, last_step), other, True):
        copy.start()

    @pl.when(step >= 2)
    def _():
        for copy in copies(step - 2, slot, False):
            copy.wait()

    def sub_tile(sub):
        t = slice(sub * FFN_SUB_STEPS, (sub + 1) * FFN_SUB_STEPS)
        act = act_ref.at[sub % act_ref.shape[0]]
        st = {}

        def conv(lo, hi):
            up_ref[carry:carry + sub_rows, lo:hi] = jnp.dot(
                st["h"], wup_ref[:, lo:hi], preferred_element_type=_F32)
            y = cb_ref[:, lo:hi]
            for tap in range(CONV_WIDTH):
                first = tap * batch
                y = y + up_ref[first:first + sub_rows, lo:hi] * cw_ref[tap:tap + 1, lo:hi]
            return y

        def block(blk):
            if blk == 0:
                x = xin_ref[slot, t].reshape(sub_rows, d)
                st["x"] = x
                st["h"] = (x * _rms_scale(x) * g_ref[...]).astype(_BF16)
                up_ref[0:carry, :] = up_ref[sub_rows:sub_rows + carry, :]
            lo, hi = blk * FF_BLOCK, (blk + 1) * FF_BLOCK
            a = conv(lo, hi)
            b = conv(D_FF + lo, D_FF + hi)
            act[:, lo:hi] = (a * jax.nn.sigmoid(a) * b).astype(_BF16)

        def finish():
            y = st["x"]
            for lo, hi in ((0, DOWN_SPLIT), (DOWN_SPLIT, D_FF)):
                y = y + jnp.dot(act[:, lo:hi], wdown_ref[lo:hi, :],
                                preferred_element_type=_F32)
            y = y * _rms_scale(y) * gf_ref[...]
            yout_ref[slot, t] = y.reshape(FFN_SUB_STEPS, batch, d)

        return [lambda blk=blk: block(blk) for blk in range(D_FF // FF_BLOCK)], finish

    pending_finish = None
    for sub in range(steps // FFN_SUB_STEPS):
        blocks, finish = sub_tile(sub)
        for n, run_block in enumerate(blocks):
            run_block()
            if n + 1 == DOWN_DELAY and pending_finish is not None:
                pending_finish()
        pending_finish = finish
    pending_finish()

    for copy in copies(step, slot, False):
        copy.start()

    @pl.when(step == last_step)
    def _():
        for copy in (copies(step - 1, other, False) + copies(step, slot, False)
                     + copies(step, other, True)):
            copy.wait()


def _const_spec(shape, grid_rank):
    zeros = (0,) * len(shape)
    index_map = (lambda j: zeros) if grid_rank == 1 else (lambda b, j: zeros)
    return pl.BlockSpec(shape, index_map, pipeline_mode=pl.Buffered(1))


def _as_f32_constants(tables):
    return tuple(jnp.asarray(t.astype(np.float32)) for t in tables)


def _decay_tables():
    log_gamma = np.log(1.0 - np.power(2.0, -5.0 - np.arange(RET_HEADS, dtype=np.float64)))
    pos = np.arange(RET_CHUNK, dtype=np.float64)
    diff = pos[:, None] - pos[None, :]
    mask = np.where(diff >= 0.0,
                    np.exp(log_gamma[:, None, None] * np.maximum(diff, 0.0)[None]), 0.0)
    k_decay = np.exp(log_gamma[:, None] * (RET_CHUNK - 1.0 - pos)[None])
    q_decay = np.exp(log_gamma[:, None] * (pos + 1.0)[None])
    chunk_decay = np.exp(log_gamma * RET_CHUNK)
    widen = lambda t: np.repeat(t.T, HEAD_DIM, axis=1)
    return _as_f32_constants(
        (mask, widen(k_decay), widen(q_decay), np.repeat(chunk_decay, HEAD_DIM)[None, :]))


def _rope_tables(seq):
    half = HEAD_DIM // 2
    inv_freq = np.power(ROPE_BASE, -np.arange(half, dtype=np.float64) / half)
    ang = np.arange(seq, dtype=np.float64)[:, None] * inv_freq[None, :]
    cos, sin = np.cos(ang), np.sin(ang)
    cos_full = np.concatenate([cos, cos], axis=-1)
    sin_signed = np.concatenate([-sin, sin], axis=-1)
    scale = HEAD_DIM ** -0.5
    return _as_f32_constants((cos_full, sin_signed, cos_full * scale, sin_signed * scale))


def kernel(x, mix_norm_g, w_in, ret_norm_g, sgu_ln_g, sgu_ln_b, sgu_w_s, sgu_b_s, w_out,
           ffn_norm_g, w_up, conv_w, conv_b, w_down, final_norm_g):
    batch, seq, d_model = x.shape
    assert w_in.shape[0] == 1
    assert d_model == D_MODEL and seq % MIX_ROWS == 0 and MIX_ROWS % CHUNK == 0
    assert batch % MIX_BATCH == 0 and seq % FFN_STEPS == 0 and FFN_STEPS % FFN_SUB_STEPS == 0
    assert D_FF % FF_BLOCK == 0
    l = 0

    cos, sin, cosk, sink = _rope_tables(seq)
    mask, kdec, qdec, cdec = _decay_tables()
    out_shape = jax.ShapeDtypeStruct(x.shape, _F32)

    mix_grid = (batch // MIX_BATCH, seq // MIX_ROWS)
    mix_steps = mix_grid[0] * mix_grid[1]
    assert D_MODEL % (16 * mix_steps) == 0 and D_FF % (16 * mix_steps) == 0
    mix_x_spec = pl.BlockSpec((MIX_BATCH, MIX_ROWS, D_MODEL), lambda b, j: (b, j, 0))
    rope_spec = pl.BlockSpec((MIX_ROWS, HEAD_DIM), lambda b, j: (j, 0))
    wup_spec = pl.BlockSpec((D_MODEL // mix_steps, 2 * D_FF),
                            lambda b, j: (b * mix_grid[1] + j, 0))
    wdown_spec = pl.BlockSpec((D_FF // mix_steps, D_MODEL),
                              lambda b, j: (b * mix_grid[1] + j, 0))
    c2 = lambda shape: _const_spec(shape, 2)
    hbm_spec = pl.BlockSpec(memory_space=pl.ANY)
    assert w_in.shape[1] % CAST_ROWS == 0 and w_out.shape[1] % CAST_ROWS == 0
    x, w_up_bf, w_down_bf = pl.pallas_call(
        _mix_kernel,
        grid=mix_grid,
        in_specs=[mix_x_spec, c2((1, D_MODEL)), hbm_spec,
                  rope_spec, rope_spec, rope_spec, rope_spec,
                  c2(mask.shape), c2(kdec.shape), c2(qdec.shape), c2(cdec.shape),
                  c2((1, RET_WIDTH)), c2(sgu_ln_g.shape[1:]), c2(sgu_ln_b.shape[1:]),
                  c2(sgu_w_s.shape[1:]), c2(sgu_b_s.shape[1:]), hbm_spec,
                  wup_spec, wdown_spec],
        out_specs=[mix_x_spec, wup_spec, wdown_spec],
        out_shape=[out_shape, jax.ShapeDtypeStruct(w_up.shape[1:], _BF16),
                   jax.ShapeDtypeStruct(w_down.shape[1:], _BF16)],
        scratch_shapes=[pltpu.VMEM((MIX_BATCH, RET_HEADS, HEAD_DIM, HEAD_DIM), _F32),
                        pltpu.VMEM((MIX_BATCH, MIX_ROWS, RET_WIDTH + SGU_WIDTH), _BF16),
                        pltpu.VMEM((SGU_GROUPS, CHUNK, CHUNK), _BF16),
                        pltpu.VMEM((CHUNK, SGU_WIDTH), _F32),
                        pltpu.VMEM((2, MIX_ROWS, w_in.shape[2]), _F32),
                        pltpu.VMEM(w_in.shape[1:], _BF16),
                        pltpu.VMEM(w_out.shape[1:], _BF16),
                        pltpu.VMEM((CAST_SLOTS, CAST_ROWS, w_in.shape[2]), _F32),
                        pltpu.VMEM((CAST_SLOTS, CAST_ROWS, w_out.shape[2]), _F32),
                        pltpu.SemaphoreType.DMA((CAST_SLOTS,)),
                        pltpu.SemaphoreType.DMA((CAST_SLOTS,))],
        compiler_params=pltpu.CompilerParams(dimension_semantics=("arbitrary", "arbitrary"),
                                             vmem_limit_bytes=VMEM_LIMIT_BYTES),
        name="token_mix",
    )(x, mix_norm_g[l][None, :], w_in[l], cos, sin, cosk, sink,
      mask, kdec, qdec, cdec, ret_norm_g[l][None, :],
      sgu_ln_g[l], sgu_ln_b[l],
      sgu_w_s[l], sgu_b_s[l], w_out[l], w_up[l], w_down[l])

    sub_rows = batch * FFN_SUB_STEPS
    assert seq // FFN_STEPS >= 2
    c1 = lambda shape: _const_spec(shape, 1)
    stage = pltpu.VMEM((2, FFN_STEPS, batch, D_MODEL), _F32)
    return pl.pallas_call(
        _ffn_kernel,
        grid=(seq // FFN_STEPS,),
        in_specs=[hbm_spec, c1((1, D_MODEL)), c1(w_up.shape[1:]),
                  c1(conv_w.shape[1:]), c1((1, 2 * D_FF)),
                  c1(w_down.shape[1:]), c1((1, D_MODEL))],
        out_specs=hbm_spec,
        out_shape=out_shape,
        scratch_shapes=[pltpu.VMEM((sub_rows + (CONV_WIDTH - 1) * batch, 2 * D_FF), _F32),
                        pltpu.VMEM((2, sub_rows, D_FF), _BF16),
                        stage, stage,
                        pltpu.SemaphoreType.DMA((2,)),
                        pltpu.SemaphoreType.DMA((2,))],
        compiler_params=pltpu.CompilerParams(dimension_semantics=("arbitrary",),
                                             vmem_limit_bytes=VMEM_LIMIT_BYTES),
        name="channel_mix",
    )(x, ffn_norm_g[l][None, :], w_up_bf, conv_w[l], conv_b[l][None, :],
      w_down_bf, final_norm_g[None, :])
```
